```python
import jax, jax.numpy as jnp
from jax import lax
import numpy as np

D_MODEL = 1024
BATCH = 16
SEQ = 2048
DEPTH = 4
DEC_BATCH = 8
DEC_SEQ = 4096
PAST_LEN = 128

GRID_W = 64
HG_HEADS = 4
HG_DK = 128
HG_DV = 128
HG_KEY_WIDTH = HG_HEADS * HG_DK
HG_WIDTH = HG_HEADS * HG_DV
HG_CHUNK = 32
NA_HEADS = 8
NA_DH = 64
NA_WIDTH = NA_HEADS * NA_DH
WIN_ROWS = 8
WIN_COLS = 16
MIX_WIDTH = HG_WIDTH + NA_WIDTH
D_FF = 256 * ((8 * D_MODEL // 3 + 255) // 256)
EPS = 1e-6
IN_SIZES = (HG_KEY_WIDTH, HG_KEY_WIDTH, HG_KEY_WIDTH, HG_WIDTH, HG_WIDTH, NA_WIDTH, NA_WIDTH, NA_WIDTH)
IN_WIDTH = sum(IN_SIZES)

kernel_name = "hybrid_hgrn2_natten_macaron_adaln_encoder"


def _rms_norm(x, g):
    xf = x.astype(jnp.float32)
    y = xf * lax.rsqrt(jnp.mean(xf * xf, axis=-1, keepdims=True) + EPS)
    return (y * g.astype(jnp.float32)).astype(x.dtype)


def _modulate(n, m):
    return n * (1 + m[:, 1]) + m[:, 0]


def _swiglu(h, w_gate, w_up, w_down):
    return (jax.nn.silu(h @ w_gate) * (h @ w_up)) @ w_down


def _lower_bounds(p):
    sm = jax.nn.softmax(p.astype(jnp.float32), axis=0)
    return jnp.cumsum(sm, axis=0) - sm[0]


def _gla_chunk_scan(q, k, v, log_f):
    B, L, H, DK = q.shape
    DV = v.shape[-1]
    n = L // HG_CHUNK

    def to_chunks(a):
        return a.reshape(B, n, HG_CHUNK, H, a.shape[-1]).transpose(1, 0, 3, 2, 4)

    incl = jnp.tril(jnp.ones((HG_CHUNK, HG_CHUNK), dtype=bool))

    def step(S, blk):
        qc, kc, vc, gc = blk
        b = jnp.cumsum(gc, axis=2)
        b_last = b[:, :, -1]
        inter = jnp.einsum('bhik,bhkv->bhiv', qc * jnp.exp(b), S)
        rel = jnp.where(incl[:, :, None], b[:, :, :, None, :] - b[:, :, None, :, :], -jnp.inf)
        scores = jnp.einsum('bhik,bhjk,bhijk->bhij', qc, kc, jnp.exp(rel))
        intra = jnp.einsum('bhij,bhjv->bhiv', scores, vc)
        S = jnp.exp(b_last)[..., None] * S + jnp.einsum(
            'bhjk,bhjv->bhkv', kc * jnp.exp(b_last[:, :, None] - b), vc)
        return S, inter + intra

    S0 = jnp.zeros((B, H, DK, DV), jnp.float32)
    _, o = lax.scan(step, S0, (to_chunks(q), to_chunks(k), to_chunks(v), to_chunks(log_f)))
    return o.transpose(1, 0, 3, 2, 4).reshape(B, L, H, DV)


def _hgrn2_bidir(hq, hf_f, hf_b, hi, hg, lb_f, lb_b, norm_g):
    B, L, _ = hq.shape
    q = jax.nn.silu(hq.astype(jnp.float32)).reshape(B, L, HG_HEADS, HG_DK)
    v = hi.astype(jnp.float32).reshape(B, L, HG_HEADS, HG_DV)
    f_f = (lb_f + (1 - lb_f) * jax.nn.sigmoid(hf_f.astype(jnp.float32))).reshape(B, L, HG_HEADS, HG_DK)
    f_b = (lb_b + (1 - lb_b) * jax.nn.sigmoid(hf_b.astype(jnp.float32))).reshape(B, L, HG_HEADS, HG_DK)
    o_f = _gla_chunk_scan(q, 1 - f_f, v, jnp.log(f_f))
    flip = lambda a: jnp.flip(a, axis=1)
    o_b = flip(_gla_chunk_scan(flip(q), flip(1 - f_b), flip(v), flip(jnp.log(f_b))))
    o = _rms_norm(o_f + o_b, norm_g).reshape(B, L, HG_WIDTH)
    return (o * jax.nn.silu(hg.astype(jnp.float32))).astype(hq.dtype)


def _neighbourhood_attention(q, k, v, rpb):
    B, L, H, dh = q.shape
    rows = L // GRID_W
    wr = min(WIN_ROWS, rows)
    grid = lambda a: a.reshape(B, rows, GRID_W, H, dh).transpose(0, 3, 1, 2, 4)
    qg, kg, vg = grid(q), grid(k), grid(v)
    qcol = np.arange(GRID_W)
    c0 = np.clip(qcol - WIN_COLS // 2, 0, GRID_W - WIN_COLS)
    col_mask = jnp.asarray((qcol[None, :] >= c0[:, None]) & (qcol[None, :] < c0[:, None] + WIN_COLS))
    rel_c = jnp.asarray(np.clip(qcol[None, :] - qcol[:, None], -(WIN_COLS - 1), WIN_COLS - 1) + WIN_COLS - 1)
    scale = dh ** -0.5

    def row_block(r):
        r0 = jnp.clip(r - wr // 2, 0, rows - wr)
        kb = lax.dynamic_slice_in_dim(kg, r0, wr, axis=2)
        vb = lax.dynamic_slice_in_dim(vg, r0, wr, axis=2)
        qr = lax.dynamic_index_in_dim(qg, r, axis=2, keepdims=False)
        s = jnp.einsum('bhqd,bhrkd->bhqrk', qr, kb).astype(jnp.float32) * scale
        rel_r = r0 + jnp.arange(wr) - r + (WIN_ROWS - 1)
        bias = rpb[:, rel_r[None, :, None], rel_c[:, None, :]].astype(jnp.float32)
        s = jnp.where(col_mask[:, None, :], s + bias[None], -jnp.inf)
        p = jax.nn.softmax(s, axis=(-2, -1)).astype(vb.dtype)
        return jnp.einsum('bhqrk,bhrkd->bhqd', p, vb)

    out = lax.map(row_block, jnp.arange(rows))
    return out.transpose(1, 0, 3, 2, 4).reshape(B, L, H * dh)


def _mixer(h, w_in_l, w_out_l, lb_f, lb_b, hg_norm_g_l, qn_g, kn_g, rpb_l):
    B, L, _ = h.shape
    proj = h @ w_in_l
    offs = np.cumsum(IN_SIZES)[:-1].tolist()
    hq, hf_f, hf_b, hi, hg, nq, nk, nv = jnp.split(proj, offs, axis=-1)
    o_hg = _hgrn2_bidir(hq, hf_f, hf_b, hi, hg, lb_f, lb_b, hg_norm_g_l)
    heads = lambda a: a.reshape(B, L, NA_HEADS, NA_DH)
    o_na = _neighbourhood_attention(_rms_norm(heads(nq), qn_g), _rms_norm(heads(nk), kn_g), heads(nv), rpb_l)
    return jnp.concatenate([o_hg, o_na], axis=-1) @ w_out_l


def _trunk(x, c, w_mod, b_mod, norm_g, ffn_w_gate, ffn_w_up, ffn_w_down, w_in, w_out,
           lbs_f, lbs_b, hg_norm_g, na_q_norm_g, na_k_norm_g, na_rpb):
    B = x.shape[0]
    for l in range(DEPTH):
        mod = (jax.nn.silu(c) @ w_mod[l] + b_mod[l]).reshape(B, 3, 3, D_MODEL)[:, :, :, None, :]
        h = _modulate(_rms_norm(x, norm_g[l, 0]), mod[:, 0])
        x = x + 0.5 * mod[:, 0, 2] * _swiglu(h, ffn_w_gate[l, 0], ffn_w_up[l, 0], ffn_w_down[l, 0])
        h = _modulate(_rms_norm(x, norm_g[l, 1]), mod[:, 1])
        x = x + mod[:, 1, 2] * _mixer(h, w_in[l], w_out[l], lbs_f[l], lbs_b[l], hg_norm_g[l],
                                      na_q_norm_g[l], na_k_norm_g[l], na_rpb[l])
        h = _modulate(_rms_norm(x, norm_g[l, 2]), mod[:, 2])
        x = x + 0.5 * mod[:, 2, 2] * _swiglu(h, ffn_w_gate[l, 1], ffn_w_up[l, 1], ffn_w_down[l, 1])
    return x


def setup_inputs(seed: int = 0) -> dict:
    key = jax.random.key(seed)
    ks = jax.random.split(key, 20)
    D = D_MODEL
    nrm = lambda k, shape, s: jax.random.normal(k, shape, jnp.float32) * s
    return {
        "x_prompt": nrm(ks[0], (BATCH, SEQ, D), 1.0),
        "x_sample": nrm(ks[1], (DEC_BATCH, DEC_SEQ, D), 1.0),
        "c_prompt": nrm(ks[2], (BATCH, D), 1.0),
        "c_sample": nrm(ks[3], (DEC_BATCH, D), 1.0),
        "w_mod": nrm(ks[4], (DEPTH, D, 9 * D), 0.5 * D ** -0.5),
        "b_mod": nrm(ks[5], (DEPTH, 9 * D), 0.02),
        "norm_g": 1.0 + nrm(ks[6], (DEPTH, 3, D), 0.02),
        "ffn_w_gate": nrm(ks[7], (DEPTH, 2, D, D_FF), D ** -0.5),
        "ffn_w_up": nrm(ks[8], (DEPTH, 2, D, D_FF), D ** -0.5),
        "ffn_w_down": nrm(ks[9], (DEPTH, 2, D_FF, D), D_FF ** -0.5),
        "w_in": nrm(ks[10], (DEPTH, D, IN_WIDTH), D ** -0.5),
        "w_out": nrm(ks[11], (DEPTH, MIX_WIDTH, D), MIX_WIDTH ** -0.5),
        "hg_lb_fwd": nrm(ks[12], (DEPTH, HG_KEY_WIDTH), 1.0),
        "hg_lb_bwd": nrm(ks[13], (DEPTH, HG_KEY_WIDTH), 1.0),
        "hg_norm_g": 1.0 + nrm(ks[14], (DEPTH, HG_DV), 0.02),
        "na_q_norm_g": 1.0 + nrm(ks[15], (DEPTH, NA_DH), 0.02),
        "na_k_norm_g": 1.0 + nrm(ks[16], (DEPTH, NA_DH), 0.02),
        "na_rpb": nrm(ks[17], (DEPTH, NA_HEADS, 2 * WIN_ROWS - 1, 2 * WIN_COLS - 1), 0.1),
    }


def reference(x_prompt, x_sample, c_prompt, c_sample, w_mod, b_mod, norm_g, ffn_w_gate, ffn_w_up,
              ffn_w_down, w_in, w_out, hg_lb_fwd, hg_lb_bwd, hg_norm_g, na_q_norm_g, na_k_norm_g, na_rpb):
    lbs_f = _lower_bounds(hg_lb_fwd)
    lbs_b = _lower_bounds(hg_lb_bwd)
    y_prompt = _trunk(x_prompt, c_prompt, w_mod, b_mod, norm_g, ffn_w_gate, ffn_w_up, ffn_w_down,
                      w_in, w_out, lbs_f, lbs_b, hg_norm_g, na_q_norm_g, na_k_norm_g, na_rpb)
    y_sample = _trunk(x_sample, c_sample, w_mod, b_mod, norm_g, ffn_w_gate, ffn_w_up, ffn_w_down,
                      w_in, w_out, lbs_f, lbs_b, hg_norm_g, na_q_norm_g, na_k_norm_g, na_rpb)
    return (y_prompt, y_sample)
```

```python
import functools

import numpy as np
import jax
import jax.numpy as jnp
from jax import lax
from jax.experimental import pallas as pl
from jax.experimental.pallas import tpu as pltpu

F32 = jnp.float32
BF16 = jnp.bfloat16

D_MODEL = 1024
DEPTH = 4
GRID_W = 64
HG_HEADS = 4
HG_DK = 128
HG_WIDTH = 512
NA_HEADS = 8
NA_DH = 64
NA_WIDTH = 512
WIN_ROWS = 8
WIN_COLS = 16
D_FF = 2816
IN_WIDTH = 4096
EPS = 1e-6

LANES = 128
TM = 512
HG_C = 128
HG_LEVELS = tuple(HG_C >> i for i in range(HG_C.bit_length() - 1))
NLEV = len(HG_LEVELS)
NEG_BIG = -1e30
VMEM_LIMIT = 56 * 1024 * 1024

NT_DIMS = (((1,), (1,)), ((), ()))
TN_DIMS = (((0,), (0,)), ((), ()))


def _dot(a, b):
    return jnp.dot(a, b, preferred_element_type=F32)


def _sigmoid(x):
    return 1.0 / (1.0 + jnp.exp(-x))


def _silu(x):
    return x * _sigmoid(x)


def _const_spec(shape):
    nd = len(shape)
    return pl.BlockSpec(shape, lambda *_: (0,) * nd, pipeline_mode=pl.Buffered(1))


def _mod_kernel(c_ref, w_ref, b_ref, o_ref):
    a = _silu(c_ref[...]).astype(BF16)
    o_ref[0] = _dot(a, w_ref[0].astype(BF16)) + b_ref[0]


def _modulation(c_all, w_mod, b_mod):
    nb = c_all.shape[0]
    tn = 1152
    return pl.pallas_call(
        _mod_kernel,
        grid=(DEPTH, 9 * D_MODEL // tn),
        in_specs=[
            pl.BlockSpec((nb, D_MODEL), lambda l, j: (0, 0)),
            pl.BlockSpec((1, D_MODEL, tn), lambda l, j: (l, 0, j)),
            pl.BlockSpec((1, 1, tn), lambda l, j: (l, 0, j)),
        ],
        out_specs=pl.BlockSpec((1, nb, tn), lambda l, j: (l, 0, j)),
        out_shape=jax.ShapeDtypeStruct((DEPTH, nb, 9 * D_MODEL), F32),
        compiler_params=pltpu.CompilerParams(
            dimension_semantics=("arbitrary", "arbitrary"), vmem_limit_bytes=VMEM_LIMIT),
        name="modulation",
    )(c_all, w_mod, b_mod.reshape(DEPTH, 1, 9 * D_MODEL))


def _norm_modulate(x, g, shift, scale):
    ms = jnp.mean(x * x, axis=-1, keepdims=True)
    h = (x * lax.rsqrt(ms + EPS)) * g
    return h * (1.0 + scale) + shift


def _ffn_kernel(*refs, sub, with_mix):
    if with_mix:
        x_ref, ohg_ref, ona_ref, wout_ref, mod_ref, ng_ref, wg_ref, wu_ref, wd_ref, o_ref = refs
    else:
        x_ref, mod_ref, ng_ref, wg_ref, wu_ref, wd_ref, o_ref = refs
    x = x_ref[...]
    if with_mix:
        mix = _dot(ohg_ref[...], wout_ref[0:HG_WIDTH, :]) + _dot(ona_ref[...], wout_ref[HG_WIDTH:, :])
        x = x + mod_ref[0, 5:6, :] * mix
    shift = mod_ref[0, 3 * sub:3 * sub + 1, :]
    scale = mod_ref[0, 3 * sub + 1:3 * sub + 2, :]
    gate = mod_ref[0, 3 * sub + 2:3 * sub + 3, :]
    h = _norm_modulate(x, ng_ref[...], shift, scale).astype(BF16)
    g = _dot(h, wg_ref[...])
    u = _dot(h, wu_ref[...])
    a = (_silu(g) * u).astype(BF16)
    y = _dot(a, wd_ref[...])
    o_ref[...] = x + 0.5 * gate * y


def _ffn(x, mod_l, ng, wg, wu, wd, *, seq, sub, mix=None):
    n = x.shape[0]
    tok = lambda w: pl.BlockSpec((TM, w), lambda i: (i, 0))
    mod_spec = pl.BlockSpec((1, 9, D_MODEL), lambda i: (i * TM // seq, 0, 0))
    args, specs = [x], [tok(D_MODEL)]
    if mix is not None:
        ohg, ona, wout = mix
        args += [ohg, ona, wout]
        specs += [tok(HG_WIDTH), tok(NA_WIDTH), _const_spec((D_MODEL, D_MODEL))]
    args += [mod_l, ng, wg, wu, wd]
    specs += [mod_spec, _const_spec((1, D_MODEL)), _const_spec((D_MODEL, D_FF)),
              _const_spec((D_MODEL, D_FF)), _const_spec((D_FF, D_MODEL))]
    return pl.pallas_call(
        functools.partial(_ffn_kernel, sub=sub, with_mix=mix is not None),
        grid=(n // TM,),
        in_specs=specs,
        out_specs=tok(D_MODEL),
        out_shape=jax.ShapeDtypeStruct((n, D_MODEL), F32),
        compiler_params=pltpu.CompilerParams(
            dimension_semantics=("arbitrary",), vmem_limit_bytes=VMEM_LIMIT),
        name="ffn_mix" if mix is not None else "ffn",
    )(*args)


def _proj_kernel(x_ref, mod_ref, ng_ref, w_ref, o_ref):
    h = _norm_modulate(x_ref[...], ng_ref[...], mod_ref[0, 3:4, :], mod_ref[0, 4:5, :])
    o_ref[...] = _dot(h.astype(BF16), w_ref[...])


def _in_proj(x, mod_l, ng, w_in, *, seq):
    n = x.shape[0]
    return pl.pallas_call(
        _proj_kernel,
        grid=(n // TM,),
        in_specs=[
            pl.BlockSpec((TM, D_MODEL), lambda i: (i, 0)),
            pl.BlockSpec((1, 9, D_MODEL), lambda i: (i * TM // seq, 0, 0)),
            _const_spec((1, D_MODEL)),
            _const_spec((D_MODEL, IN_WIDTH)),
        ],
        out_specs=pl.BlockSpec((TM, IN_WIDTH), lambda i: (i, 0)),
        out_shape=jax.ShapeDtypeStruct((n, IN_WIDTH), F32),
        compiler_params=pltpu.CompilerParams(
            dimension_semantics=("arbitrary",), vmem_limit_bytes=VMEM_LIMIT),
        name="in_proj",
    )(x, mod_l, ng, w_in)


def _hgrn_constants():
    c = HG_C
    idx = np.arange(c)
    t = idx[None, :]
    i = idx[:, None]
    blocks = [(t <= i), (t > i)]
    masks, sels = [], []
    for s in HG_LEVELS:
        half = s // 2
        p = idx % s
        start = idx - p
        m = (start + half - 1)[:, None]
        is_q = (p >= half)
        qrow = (t >= m + 1) & (t <= i)
        krow = (t >= i + 1) & (t <= m)
        blocks.append(np.where(is_q[:, None], qrow, krow))
        masks.append(is_q[:, None] & ~is_q[None, :] & (start[:, None] == start[None, :]))
        sels.append(np.broadcast_to(is_q[:, None], (c, LANES)))
    fwd_m = np.stack(blocks).astype(np.float32)
    fwd_k = np.stack(masks).astype(np.float32)
    fwd_s = np.stack(sels).astype(np.float32)
    mstack = np.stack([fwd_m, fwd_m[:, ::-1, ::-1]]).reshape(2, (2 + NLEV) * c, c)
    mask = np.stack([fwd_k, fwd_k[:, ::-1, ::-1]])
    sel = np.stack([fwd_s, fwd_s[:, ::-1, :]])
    return (jnp.asarray(mstack, dtype=BF16), jnp.asarray(mask, dtype=F32), jnp.asarray(sel, dtype=F32))


def _hgrn_chunk(d, rows, st_ref, hq_ref, hf_ref, hi_ref, lb, mst_ref, msk_ref, sel_ref):
    c = HG_C
    q = _silu(hq_ref[rows, :])
    f = lb + (1.0 - lb) * _sigmoid(hf_ref[rows, :])
    g = jnp.log(f)
    kk = 1.0 - f
    v = hi_ref[rows, :]
    vb = v.astype(BF16)
    g1 = g.astype(BF16)
    r1 = g - g1.astype(F32)
    g2 = r1.astype(BF16)
    g3 = (r1 - g2.astype(F32)).astype(BF16)
    m = mst_ref[d]
    e = jnp.exp(_dot(m, g1) + _dot(m, g2) + _dot(m, g3))
    qe = (q * e[0:c]).astype(BF16)
    ke = (kk * e[c:2 * c]).astype(BF16)
    dqk = q - kk
    p = jnp.zeros((c, c), F32)
    for l in range(NLEV):
        x = ((kk + sel_ref[d, l] * dqk) * e[(2 + l) * c:(3 + l) * c]).astype(BF16)
        p = p + lax.dot_general(x, x, NT_DIMS, preferred_element_type=F32) * msk_ref[d, l]
    diag = jnp.sum(q * kk, axis=-1, keepdims=True)
    st = st_ref[d]
    o = _dot(p.astype(BF16), vb) + diag * v
    o = o + lax.dot_general(qe, st.astype(BF16), NT_DIMS, preferred_element_type=F32)
    last = c - 1 if d == 0 else 0
    e_tot = e[last:last + 1]
    st_ref[d] = st * e_tot + lax.dot_general(vb, ke, TN_DIMS, preferred_element_type=F32)
    return o


def _hgrn_kernel(hq_ref, hff_ref, hfb_ref, hi_ref, hg_ref, lbf_ref, lbb_ref, ng_ref,
                 mst_ref, msk_ref, sel_ref, o_ref, of_ref, ob_ref, st_ref, *, seq):
    n = seq // HG_C
    st_ref[...] = jnp.zeros_like(st_ref)
    lbf = lbf_ref[...]
    lbb = lbb_ref[...]

    def body(t, carry):
        rf = pl.ds(pl.multiple_of(t * HG_C, HG_C), HG_C)
        rb = pl.ds(pl.multiple_of((n - 1 - t) * HG_C, HG_C), HG_C)
        of_ref[rf, :] = _hgrn_chunk(0, rf, st_ref, hq_ref, hff_ref, hi_ref, lbf, mst_ref, msk_ref, sel_ref)
        ob_ref[rb, :] = _hgrn_chunk(1, rb, st_ref, hq_ref, hfb_ref, hi_ref, lbb, mst_ref, msk_ref, sel_ref)
        return carry

    lax.fori_loop(0, n, body, 0)

    def finish(t, carry):
        rows = pl.ds(pl.multiple_of(t * HG_C, HG_C), HG_C)
        o = of_ref[rows, :] + ob_ref[rows, :]
        ms = jnp.mean(o * o, axis=-1, keepdims=True)
        o = (o * lax.rsqrt(ms + EPS)) * ng_ref[...]
        o_ref[rows, :] = (o * _silu(hg_ref[rows, :])).astype(BF16)
        return carry

    lax.fori_loop(0, n, finish, 0)


def _hgrn(proj, lb_f, lb_b, ng, consts, *, seq):
    n = proj.shape[0]
    nb = n // seq
    mst, msk, sel = consts
    col = lambda off: pl.BlockSpec((seq, LANES), lambda b, h: (b, off + h))
    vec = pl.BlockSpec((1, LANES), lambda b, h: (0, h))
    return pl.pallas_call(
        functools.partial(_hgrn_kernel, seq=seq),
        grid=(nb, HG_HEADS),
        in_specs=[col(0), col(4), col(8), col(12), col(16), vec, vec, _const_spec((1, LANES)),
                  _const_spec(mst.shape), _const_spec(msk.shape), _const_spec(sel.shape)],
        out_specs=pl.BlockSpec((seq, LANES), lambda b, h: (b, h)),
        out_shape=jax.ShapeDtypeStruct((n, HG_WIDTH), BF16),
        scratch_shapes=[pltpu.VMEM((seq, LANES), F32), pltpu.VMEM((seq, LANES), F32),
                        pltpu.VMEM((2, LANES, HG_DK), F32)],
        compiler_params=pltpu.CompilerParams(
            dimension_semantics=("arbitrary", "arbitrary"), vmem_limit_bytes=VMEM_LIMIT),
        name="hgrn2",
    )(proj, proj, proj, proj, proj, lb_f, lb_b, ng, mst, msk, sel)


NA_NORM_ROWS = 512


def _na_kernel(q_ref, k_ref, v_ref, qg_ref, kg_ref, bias_ref, bd_ref, o_ref, qs_ref, ks_ref, vs_ref, *, seq):
    rows = seq // GRID_W
    win = WIN_ROWS * GRID_W
    bd = bd_ref[...]

    def head_norm(x, g):
        x2 = x * x
        hi = x2.astype(BF16)
        lo = (x2 - hi.astype(F32)).astype(BF16)
        ms = (_dot(hi, bd) + _dot(lo, bd)) * (1.0 / NA_DH)
        return (x * lax.rsqrt(ms + EPS)) * g

    def prep(t, carry):
        r = pl.ds(pl.multiple_of(t * NA_NORM_ROWS, NA_NORM_ROWS), NA_NORM_ROWS)
        qs_ref[r, :] = (head_norm(q_ref[r, :], qg_ref[...]) * (NA_DH ** -0.5)).astype(BF16)
        ks_ref[r, :] = head_norm(k_ref[r, :], kg_ref[...]).astype(BF16)
        vs_ref[r, :] = v_ref[r, :].astype(BF16)
        return carry

    lax.fori_loop(0, seq // NA_NORM_ROWS, prep, 0)

    lane = lax.broadcasted_iota(jnp.int32, (GRID_W, LANES), 1)
    first = lane < NA_DH

    def body(r, carry):
        r0 = jnp.clip(r - WIN_ROWS // 2, 0, rows - WIN_ROWS)
        dl = r - r0
        qr = qs_ref[pl.ds(pl.multiple_of(r * GRID_W, GRID_W), GRID_W), :]
        wrows = pl.ds(pl.multiple_of(r0 * GRID_W, GRID_W), win)
        kw = ks_ref[wrows, :]
        vw = vs_ref[wrows, :]
        outs = []
        for a in range(2):
            qa = jnp.where(first if a == 0 else jnp.logical_not(first), qr, jnp.zeros_like(qr))
            s = lax.dot_general(qa, kw, NT_DIMS, preferred_element_type=F32) + bias_ref[0, dl, a]
            m = jnp.max(s, axis=-1, keepdims=True)
            p = jnp.exp(s - m)
            l = jnp.sum(p, axis=-1, keepdims=True)
            outs.append(_dot(p.astype(BF16), vw) * (1.0 / l))
        o = jnp.where(first, outs[0], outs[1])
        o_ref[pl.ds(pl.multiple_of(r * GRID_W, GRID_W), GRID_W), :] = o.astype(BF16)
        return carry

    lax.fori_loop(0, rows, body, 0)


def _na_bias_tables(na_rpb):
    qcol = np.arange(GRID_W)
    c0 = np.clip(qcol - WIN_COLS // 2, 0, GRID_W - WIN_COLS)
    col_mask = (qcol[None, :] >= c0[:, None]) & (qcol[None, :] < c0[:, None] + WIN_COLS)
    rel_c = np.clip(qcol[None, :] - qcol[:, None], -(WIN_COLS - 1), WIN_COLS - 1) + WIN_COLS - 1
    rel_r = np.arange(WIN_ROWS)[None, :] - np.arange(WIN_ROWS)[:, None] + WIN_ROWS - 1
    b = na_rpb[:, :, rel_r[:, None, :, None], rel_c[None, :, None, :]]
    b = jnp.where(col_mask[None, None, None, :, None, :], b.astype(F32), NEG_BIG)
    b = b.reshape(DEPTH, NA_HEADS // 2, 2, WIN_ROWS, GRID_W, WIN_ROWS * GRID_W)
    return b.transpose(0, 1, 3, 2, 4, 5)


def _na(proj, qg, kg, bias_l, bd, *, seq):
    n = proj.shape[0]
    nb = n // seq
    col = lambda off: pl.BlockSpec((seq, LANES), lambda hp, b: (b, off + hp))
    return pl.pallas_call(
        functools.partial(_na_kernel, seq=seq),
        grid=(NA_HEADS // 2, nb),
        in_specs=[col(20), col(24), col(28), _const_spec((1, LANES)), _const_spec((1, LANES)),
                  pl.BlockSpec((1, WIN_ROWS, 2, GRID_W, WIN_ROWS * GRID_W), lambda hp, b: (hp, 0, 0, 0, 0)),
                  _const_spec((LANES, LANES))],
        out_specs=pl.BlockSpec((seq, LANES), lambda hp, b: (b, hp)),
        out_shape=jax.ShapeDtypeStruct((n, NA_WIDTH), BF16),
        scratch_shapes=[pltpu.VMEM((seq, LANES), BF16)] * 3,
        compiler_params=pltpu.CompilerParams(
            dimension_semantics=("arbitrary", "arbitrary"), vmem_limit_bytes=VMEM_LIMIT),
        name="natten",
    )(proj, proj, proj, qg, kg, bias_l, bd)


def _lower_bounds(p):
    sm = jax.nn.softmax(p.astype(F32), axis=0)
    return jnp.cumsum(sm, axis=0) - sm[0]


def _trunk(x, mod, seq, w, consts):
    nb = x.shape[0]
    x = x.reshape(nb * seq, D_MODEL)
    for l in range(DEPTH):
        mod_l = mod[l]
        x = _ffn(x, mod_l, w["norm_g"][l, 0], w["wg"][l, 0], w["wu"][l, 0], w["wd"][l, 0], seq=seq, sub=0)
        proj = _in_proj(x, mod_l, w["norm_g"][l, 1], w["w_in"][l], seq=seq)
        ohg = _hgrn(proj, w["lb_f"][l], w["lb_b"][l], w["hg_norm_g"][l], consts["hgrn"], seq=seq)
        ona = _na(proj, w["qg"][l], w["kg"][l], w["bias"][l], consts["bd"], seq=seq)
        x = _ffn(x, mod_l, w["norm_g"][l, 2], w["wg"][l, 1], w["wu"][l, 1], w["wd"][l, 1], seq=seq, sub=2,
                 mix=(ohg, ona, w["w_out"][l]))
    return x.reshape(nb, seq, D_MODEL)


def kernel(x_prompt, x_sample, c_prompt, c_sample, w_mod, b_mod, norm_g, ffn_w_gate, ffn_w_up, ffn_w_down,
           w_in, w_out, hg_lb_fwd, hg_lb_bwd, hg_norm_g, na_q_norm_g, na_k_norm_g, na_rpb):
    nbp = x_prompt.shape[0]
    c_all = jnp.concatenate([c_prompt, c_sample], axis=0)
    mod = _modulation(c_all, w_mod, b_mod).reshape(DEPTH, c_all.shape[0], 9, D_MODEL)
    two = lambda g: jnp.concatenate([g, g], axis=-1).reshape(DEPTH, 1, LANES)
    w = {
        "norm_g": norm_g.reshape(DEPTH, 3, 1, D_MODEL),
        "wg": ffn_w_gate.astype(BF16), "wu": ffn_w_up.astype(BF16), "wd": ffn_w_down.astype(BF16),
        "w_in": w_in.astype(BF16), "w_out": w_out.astype(BF16),
        "lb_f": _lower_bounds(hg_lb_fwd).reshape(DEPTH, 1, HG_WIDTH),
        "lb_b": _lower_bounds(hg_lb_bwd).reshape(DEPTH, 1, HG_WIDTH),
        "hg_norm_g": hg_norm_g.reshape(DEPTH, 1, LANES),
        "qg": two(na_q_norm_g), "kg": two(na_k_norm_g),
        "bias": _na_bias_tables(na_rpb),
    }
    half = np.arange(LANES) // NA_DH
    consts = {
        "hgrn": _hgrn_constants(),
        "bd": jnp.asarray(half[:, None] == half[None, :], dtype=BF16),
    }
    y_prompt = _trunk(x_prompt, mod[:, :nbp], x_prompt.shape[1], w, consts)
    y_sample = _trunk(x_sample, mod[:, nbp:], x_sample.shape[1], w, consts)
    return (y_prompt, y_sample)
```

```python
import functools
import math

import numpy as np
import jax
import jax.numpy as jnp
from jax import lax
from jax.experimental import pallas as pl
from jax.experimental.pallas import tpu as pltpu

F32 = jnp.float32
BF16 = jnp.bfloat16

D_MODEL = 1024
DEPTH = 4
GRID_W = 64
HG_HEADS = 4
HG_DK = 128
HG_WIDTH = 512
NA_HEADS = 8
NA_DH = 64
NA_WIDTH = 512
WIN_ROWS = 8
WIN_COLS = 16
D_FF = 2816
IN_WIDTH = 4096
EPS = 1e-6
LOG2E = math.log2(math.e)

LANES = 128
SUBLANES = 8
TM = 512
HG_C = 128
HG_R = 512
HG_LEVELS = tuple(HG_C >> i for i in range(HG_C.bit_length() - 1))
HG_COARSE = tuple(s for s in HG_LEVELS if s // 2 >= SUBLANES)
HG_FINE = tuple(s for s in HG_LEVELS if s // 2 < SUBLANES)
NA_ROWS_PER_ITER = 4
NEG_BIG = -1e30
VMEM_LIMIT = 56 * 1024 * 1024

NT_DIMS = (((1,), (1,)), ((), ()))
TN_DIMS = (((0,), (0,)), ((), ()))


def _dot(a, b):
    return jnp.dot(a, b, preferred_element_type=F32)


def _dot_nt(a, b):
    return lax.dot_general(a, b, NT_DIMS, preferred_element_type=F32)


def _sigmoid(x):
    return 1.0 / (1.0 + jnp.exp(-x))


def _silu(x):
    return x * _sigmoid(x)


def _const_spec(shape):
    nd = len(shape)
    return pl.BlockSpec(shape, lambda *_: (0,) * nd, pipeline_mode=pl.Buffered(1))


def _params(*sem):
    return pltpu.CompilerParams(dimension_semantics=sem, vmem_limit_bytes=VMEM_LIMIT)


def _mod_kernel(c_ref, w_ref, b_ref, o_ref):
    a = _silu(c_ref[...]).astype(BF16)
    o_ref[0] = _dot(a, w_ref[0].astype(BF16)) + b_ref[0]


def _modulation(c_all, w_mod, b_mod):
    nb = c_all.shape[0]
    tn = 1152
    return pl.pallas_call(
        _mod_kernel,
        grid=(DEPTH, 9 * D_MODEL // tn),
        in_specs=[
            pl.BlockSpec((nb, D_MODEL), lambda l, j: (0, 0)),
            pl.BlockSpec((1, D_MODEL, tn), lambda l, j: (l, 0, j)),
            pl.BlockSpec((1, 1, tn), lambda l, j: (l, 0, j)),
        ],
        out_specs=pl.BlockSpec((1, nb, tn), lambda l, j: (l, 0, j)),
        out_shape=jax.ShapeDtypeStruct((DEPTH, nb, 9 * D_MODEL), F32),
        compiler_params=_params("arbitrary", "arbitrary"),
        name="modulation",
    )(c_all, w_mod, b_mod.reshape(DEPTH, 1, 9 * D_MODEL))


def _norm_modulate(x, g, shift, scale):
    ms = jnp.mean(x * x, axis=-1, keepdims=True)
    h = (x * lax.rsqrt(ms + EPS)) * g
    return h * (1.0 + scale) + shift


def _ffn_kernel(*refs, sub, with_mix):
    if with_mix:
        (x_ref, of_ref, ob_ref, sg_ref, ona_ref, hng_ref, wout_ref,
         mod_ref, ng_ref, wg_ref, wu_ref, wd_ref, o_ref) = refs
    else:
        x_ref, mod_ref, ng_ref, wg_ref, wu_ref, wd_ref, o_ref = refs
    x = x_ref[...]
    if with_mix:
        heads = []
        for h in range(HG_HEADS):
            cols = slice(h * LANES, (h + 1) * LANES)
            o = of_ref[:, cols] + ob_ref[:, cols]
            ms = jnp.mean(o * o, axis=-1, keepdims=True)
            o = (o * lax.rsqrt(ms + EPS)) * hng_ref[...]
            heads.append((o * sg_ref[:, cols].astype(F32)).astype(BF16))
        ohg = jnp.concatenate(heads, axis=-1)
        mix = _dot(ohg, wout_ref[0:HG_WIDTH, :]) + _dot(ona_ref[...], wout_ref[HG_WIDTH:, :])
        x = x + mod_ref[0, 5:6, :] * mix
    shift = mod_ref[0, 3 * sub:3 * sub + 1, :]
    scale = mod_ref[0, 3 * sub + 1:3 * sub + 2, :]
    gate = mod_ref[0, 3 * sub + 2:3 * sub + 3, :]
    h = _norm_modulate(x, ng_ref[...], shift, scale).astype(BF16)
    g = _dot(h, wg_ref[...])
    u = _dot(h, wu_ref[...])
    a = (_silu(g) * u).astype(BF16)
    y = _dot(a, wd_ref[...])
    o_ref[...] = x + 0.5 * gate * y


def _ffn(x, mod_l, ng, wg, wu, wd, *, seq, sub, mix=None):
    n = x.shape[0]
    tok = lambda w: pl.BlockSpec((TM, w), lambda i: (i, 0))
    mod_spec = pl.BlockSpec((1, 9, D_MODEL), lambda i: (i * TM // seq, 0, 0))
    args, specs = [x], [tok(D_MODEL)]
    if mix is not None:
        of, ob, sg, ona, hng, wout = mix
        args += [of, ob, sg, ona, hng, wout]
        specs += [tok(HG_WIDTH), tok(HG_WIDTH), tok(HG_WIDTH), tok(NA_WIDTH),
                  _const_spec((1, LANES)), _const_spec((D_MODEL, D_MODEL))]
    args += [mod_l, ng, wg, wu, wd]
    specs += [mod_spec, _const_spec((1, D_MODEL)), _const_spec((D_MODEL, D_FF)),
              _const_spec((D_MODEL, D_FF)), _const_spec((D_FF, D_MODEL))]
    return pl.pallas_call(
        functools.partial(_ffn_kernel, sub=sub, with_mix=mix is not None),
        grid=(n // TM,),
        in_specs=specs,
        out_specs=tok(D_MODEL),
        out_shape=jax.ShapeDtypeStruct((n, D_MODEL), F32),
        compiler_params=_params("arbitrary"),
        name="ffn_mix" if mix is not None else "ffn",
    )(*args)


PROJ_OUTS = ("q", "kf", "kb", "gfh", "gfl", "gbh", "gbl", "v", "sg", "nq", "nk", "nv")


def _proj_kernel(x_ref, mod_ref, ng_ref, w_ref, lbf_ref, lbb_ref, qg_ref, kg_ref, *outs):
    o = dict(zip(PROJ_OUTS, outs))
    h = _norm_modulate(x_ref[...], ng_ref[...], mod_ref[0, 3:4, :], mod_ref[0, 4:5, :]).astype(BF16)
    seg = lambda i: _dot(h, w_ref[:, i * HG_WIDTH:(i + 1) * HG_WIDTH])

    o["q"][...] = _silu(seg(0)).astype(BF16)

    def decay(i, lb_ref, k_ref, hi_ref, lo_ref):
        lb = lb_ref[...]
        f = lb + (1.0 - lb) * _sigmoid(seg(i))
        k_ref[...] = (1.0 - f).astype(BF16)
        g = jnp.log2(f)
        hi = g.astype(BF16)
        hi_ref[...] = hi
        lo_ref[...] = (g - hi.astype(F32)).astype(BF16)

    decay(1, lbf_ref, o["kf"], o["gfh"], o["gfl"])
    decay(2, lbb_ref, o["kb"], o["gbh"], o["gbl"])
    o["v"][...] = seg(3).astype(BF16)
    o["sg"][...] = _silu(seg(4)).astype(BF16)

    lane = lax.broadcasted_iota(jnp.int32, (TM, LANES), 1)
    first = lane < NA_DH

    def head_norm(i, g_ref, o_ref):
        x = seg(i)
        for t in range(NA_WIDTH // LANES):
            cols = slice(t * LANES, (t + 1) * LANES)
            xt = x[:, cols]
            x2 = xt * xt
            sa = jnp.sum(jnp.where(first, x2, 0.0), axis=-1, keepdims=True)
            sb = jnp.sum(jnp.where(first, 0.0, x2), axis=-1, keepdims=True)
            ms = jnp.where(first, sa, sb) * (1.0 / NA_DH)
            o_ref[:, cols] = ((xt * lax.rsqrt(ms + EPS)) * g_ref[:, cols]).astype(BF16)

    head_norm(5, qg_ref, o["nq"])
    head_norm(6, kg_ref, o["nk"])
    o["nv"][...] = seg(7).astype(BF16)


def _in_proj(x, mod_l, ng, w_in, lb_f, lb_b, qg, kg, *, seq):
    n = x.shape[0]
    vec = _const_spec((1, HG_WIDTH))
    outs = pl.pallas_call(
        _proj_kernel,
        grid=(n // TM,),
        in_specs=[
            pl.BlockSpec((TM, D_MODEL), lambda i: (i, 0)),
            pl.BlockSpec((1, 9, D_MODEL), lambda i: (i * TM // seq, 0, 0)),
            _const_spec((1, D_MODEL)),
            _const_spec((D_MODEL, IN_WIDTH)),
            vec, vec, vec, vec,
        ],
        out_specs=[pl.BlockSpec((TM, HG_WIDTH), lambda i: (i, 0))] * len(PROJ_OUTS),
        out_shape=[jax.ShapeDtypeStruct((n, HG_WIDTH), BF16)] * len(PROJ_OUTS),
        compiler_params=_params("arbitrary"),
        name="in_proj",
    )(x, mod_l, ng, w_in, lb_f, lb_b, qg, kg)
    return dict(zip(PROJ_OUTS, outs))


def _hgrn_constants():
    c = HG_C
    idx = np.arange(c)
    t = idx[None, :]
    i = idx[:, None]
    run_blocks = [t <= i]
    sels, masks = [], []
    for s in HG_LEVELS:
        half = s // 2
        p = idx % s
        start = idx - p
        m = (start + half - 1)[:, None]
        is_q = p >= half
        masks.append(is_q[:, None] & ~is_q[None, :] & (start[:, None] == start[None, :]))
        if s in HG_FINE:
            qrow = (t >= m + 1) & (t <= i)
            krow = (t >= i + 1) & (t <= m)
            run_blocks.append(np.where(is_q[:, None], qrow, krow))
            sels.append(np.broadcast_to(is_q[:, None], (c, LANES)))
    masks.append(np.eye(c, dtype=bool))
    assert len(masks) % 2 == 0
    both = lambda a: np.stack([a, a[:, ::-1, ::-1]])
    run = both(np.stack(run_blocks).astype(np.float32))
    run = np.concatenate([run, run], axis=-1).reshape(2, -1, 2 * c)
    msk = both(np.stack(masks).astype(np.float32))
    msk = msk.reshape(2, len(masks) // 2, 2, c, c).transpose(0, 1, 3, 2, 4).reshape(2, -1, c, 2 * c)
    sel = np.stack(sels).astype(np.float32)
    sel = np.stack([sel, sel[:, ::-1, :]])
    return jnp.asarray(run, dtype=BF16), jnp.asarray(msk, dtype=F32), jnp.asarray(sel, dtype=F32)


def _hgrn_operands(d, b, a_fine, q, kk, selq):
    c = HG_C
    xs = []
    for s in HG_COARSE:
        half = s // 2
        parts = []
        for blk in range(c // half):
            rows = slice(blk * half, (blk + 1) * half)
            base = (blk // 2) * s
            if d == 0:
                m, is_q = base + half - 1, blk % 2 == 1
            else:
                m, is_q = base + half, blk % 2 == 0
            bm = b[m:m + 1]
            if is_q:
                parts.append(q[rows] * jnp.exp2(b[rows] - bm))
            else:
                parts.append(kk[rows] * jnp.exp2(bm - b[rows]))
        xs.append(jnp.concatenate(parts, axis=0).astype(BF16))
    for lf in range(len(HG_FINE)):
        e = jnp.exp2(a_fine[lf * c:(lf + 1) * c])
        xs.append((jnp.where(selq[lf], q, kk) * e).astype(BF16))
    last = c - 1 if d == 0 else 0
    b_tot = b[last:last + 1]
    qe = (q * jnp.exp2(b)).astype(BF16)
    ke = (kk * jnp.exp2(b_tot - b)).astype(BF16)
    return xs + [q.astype(BF16)], xs + [kk.astype(BF16)], qe, ke, jnp.exp2(b_tot)


def _hgrn_kernel(qf_ref, kf_ref, gfh_ref, gfl_ref, vf_ref, qb_ref, kb_ref, gbh_ref, gbl_ref, vb_ref,
                 run_ref, msk_ref, sel_ref, of_ref, ob_ref, st_ref):
    c = HG_C
    nchunk = HG_R // c
    npair = (len(HG_LEVELS) + 1) // 2

    @pl.when(pl.program_id(1) == 0)
    def _():
        st_ref[...] = jnp.zeros_like(st_ref)

    dirs = ((qf_ref, kf_ref, gfh_ref, gfl_ref, vf_ref, of_ref),
            (qb_ref, kb_ref, gbh_ref, gbl_ref, vb_ref, ob_ref))
    zero = jnp.zeros((c, LANES), BF16)

    def body(j, carry):
        chains = []
        for d, (q_ref, k_ref, gh_ref, gl_ref, v_ref, o_ref) in enumerate(dirs):
            jj = j if d == 0 else nchunk - 1 - j
            rows = pl.ds(pl.multiple_of(jj * c, c), c)
            for hp in range(HG_HEADS // 2):
                cols2 = slice(hp * 2 * LANES, (hp + 1) * 2 * LANES)
                g2 = jnp.concatenate([gh_ref[rows, cols2], gl_ref[rows, cols2]], axis=0)
                a2 = _dot(run_ref[d], g2)
                for hh in range(2):
                    lanes = slice(hh * LANES, (hh + 1) * LANES)
                    chains.append(dict(d=d, h=2 * hp + hh, rows=rows, b=a2[0:c, lanes], a_fine=a2[c:, lanes],
                                       q_ref=q_ref, k_ref=k_ref, v_ref=v_ref, o_ref=o_ref))
        selq = [[sel_ref[d, lf] != 0.0 for lf in range(len(HG_FINE))] for d in range(2)]
        for ch in chains:
            cols = slice(ch["h"] * LANES, (ch["h"] + 1) * LANES)
            ch["cols"] = cols
            ch["lhs"], ch["rhs"], ch["qe"], ch["ke"], ch["e_tot"] = _hgrn_operands(
                ch["d"], ch["b"], ch["a_fine"], ch["q_ref"][ch["rows"], cols].astype(F32),
                ch["k_ref"][ch["rows"], cols].astype(F32), selq[ch["d"]])
            ch["p"] = None
        for i in range(npair):
            for ch in chains:
                lhs, rhs = ch["lhs"], ch["rhs"]
                l2 = jnp.concatenate([lhs[2 * i], lhs[2 * i + 1]], axis=1)
                r2 = jnp.concatenate([jnp.concatenate([rhs[2 * i], zero], axis=1),
                                      jnp.concatenate([zero, rhs[2 * i + 1]], axis=1)], axis=0)
                g2 = _dot_nt(l2, r2) * msk_ref[ch["d"], i]
                g = g2[:, :c] + g2[:, c:]
                ch["p"] = g if ch["p"] is None else ch["p"] + g
        for ch in chains:
            d, h = ch["d"], ch["h"]
            vb = ch["v_ref"][ch["rows"], ch["cols"]]
            st = st_ref[d, h]
            o = _dot(ch["p"].astype(BF16), vb) + _dot_nt(ch["qe"], st.astype(BF16))
            ch["o_ref"][ch["rows"], ch["cols"]] = o
            st_ref[d, h] = st * ch["e_tot"] + lax.dot_general(vb, ch["ke"], TN_DIMS, preferred_element_type=F32)
        return carry

    lax.fori_loop(0, nchunk, body, 0)


def _hgrn(p, consts, *, seq):
    n = p["q"].shape[0]
    nb, nt = n // seq, seq // HG_R
    run, msk, sel = consts
    fwd = pl.BlockSpec((HG_R, HG_WIDTH), lambda b, t: (b * nt + t, 0))
    bwd = pl.BlockSpec((HG_R, HG_WIDTH), lambda b, t: (b * nt + nt - 1 - t, 0))
    out = jax.ShapeDtypeStruct((n, HG_WIDTH), F32)
    return pl.pallas_call(
        _hgrn_kernel,
        grid=(nb, nt),
        in_specs=[fwd] * 5 + [bwd] * 5 + [_const_spec(run.shape), _const_spec(msk.shape), _const_spec(sel.shape)],
        out_specs=[fwd, bwd],
        out_shape=[out, out],
        scratch_shapes=[pltpu.VMEM((2, HG_HEADS, LANES, HG_DK), F32)],
        compiler_params=_params("arbitrary", "arbitrary"),
        name="hgrn2",
    )(p["q"], p["kf"], p["gfh"], p["gfl"], p["v"], p["q"], p["kb"], p["gbh"], p["gbl"], p["v"], run, msk, sel)


def _na_kernel(q_ref, k_ref, v_ref, bias_ref, o_ref, *, seq):
    rows = seq // GRID_W
    win = WIN_ROWS * GRID_W
    lane = lax.broadcasted_iota(jnp.int32, (GRID_W, LANES), 1)
    first = lane < NA_DH

    def body(i, carry):
        rs = [i * NA_ROWS_PER_ITER + u for u in range(NA_ROWS_PER_ITER)]
        r0s = [jnp.clip(r - WIN_ROWS // 2, 0, rows - WIN_ROWS) for r in rs]
        qrows = [pl.ds(pl.multiple_of(r * GRID_W, GRID_W), GRID_W) for r in rs]
        wrows = [pl.ds(pl.multiple_of(r0 * GRID_W, GRID_W), win) for r0 in r0s]
        ss = []
        for r, r0, qr_, wr_ in zip(rs, r0s, qrows, wrows):
            qr = q_ref[qr_, :]
            zero = jnp.zeros_like(qr)
            q2 = jnp.concatenate([jnp.where(first, qr, zero), jnp.where(first, zero, qr)], axis=0)
            ss.append(_dot_nt(q2, k_ref[wr_, :]) + bias_ref[0, r - r0])
        ps, ls = [], []
        for s in ss:
            p = jnp.exp2(s - jnp.max(s, axis=-1, keepdims=True))
            ls.append(jnp.sum(p, axis=-1, keepdims=True))
            ps.append(p.astype(BF16))
        for p, l, qr_, wr_ in zip(ps, ls, qrows, wrows):
            o2 = _dot(p, v_ref[wr_, :]) * (1.0 / l)
            o_ref[qr_, :] = jnp.where(first, o2[:GRID_W], o2[GRID_W:]).astype(BF16)
        return carry

    lax.fori_loop(0, rows // NA_ROWS_PER_ITER, body, 0)


def _na_bias_tables(na_rpb):
    qcol = np.arange(GRID_W)
    c0 = np.clip(qcol - WIN_COLS // 2, 0, GRID_W - WIN_COLS)
    col_mask = (qcol[None, :] >= c0[:, None]) & (qcol[None, :] < c0[:, None] + WIN_COLS)
    rel_c = np.clip(qcol[None, :] - qcol[:, None], -(WIN_COLS - 1), WIN_COLS - 1) + WIN_COLS - 1
    onehot = (rel_c[:, :, None] == np.arange(2 * WIN_COLS - 1)).astype(np.float32)
    t = jnp.einsum("qkc,lhrc->lhrqk", onehot, na_rpb.astype(F32), precision=lax.Precision.HIGHEST)
    t = jnp.where(col_mask, t * LOG2E, NEG_BIG)
    per_off = [t[:, :, WIN_ROWS - 1 - dl:2 * WIN_ROWS - 1 - dl] for dl in range(WIN_ROWS)]
    b = jnp.stack(per_off, axis=2)
    b = b.transpose(0, 1, 2, 4, 3, 5).reshape(DEPTH, NA_HEADS // 2, 2, WIN_ROWS, GRID_W, WIN_ROWS * GRID_W)
    return b.transpose(0, 1, 3, 2, 4, 5).reshape(DEPTH, NA_HEADS // 2, WIN_ROWS, 2 * GRID_W, WIN_ROWS * GRID_W)


def _na(p, bias_l, *, seq):
    n = p["nq"].shape[0]
    nb = n // seq
    col = pl.BlockSpec((seq, LANES), lambda hp, b: (b, hp))
    return pl.pallas_call(
        functools.partial(_na_kernel, seq=seq),
        grid=(NA_HEADS // 2, nb),
        in_specs=[col, col, col,
                  pl.BlockSpec((1, WIN_ROWS, 2 * GRID_W, WIN_ROWS * GRID_W), lambda hp, b: (hp, 0, 0, 0))],
        out_specs=col,
        out_shape=jax.ShapeDtypeStruct((n, NA_WIDTH), BF16),
        compiler_params=_params("arbitrary", "arbitrary"),
        name="natten",
    )(p["nq"], p["nk"], p["nv"], bias_l)


def _lower_bounds(p):
    sm = jax.nn.softmax(p.astype(F32), axis=0)
    return jnp.cumsum(sm, axis=0) - sm[0]


def _trunk(x, mod, seq, w, consts):
    nb = x.shape[0]
    x = x.reshape(nb * seq, D_MODEL)
    for l in range(DEPTH):
        mod_l = mod[l]
        x = _ffn(x, mod_l, w["norm_g"][l, 0], w["wg"][l, 0], w["wu"][l, 0], w["wd"][l, 0], seq=seq, sub=0)
        p = _in_proj(x, mod_l, w["norm_g"][l, 1], w["w_in"][l], w["lb_f"][l], w["lb_b"][l],
                     w["qg"][l], w["kg"][l], seq=seq)
        of, ob = _hgrn(p, consts, seq=seq)
        ona = _na(p, w["bias"][l], seq=seq)
        x = _ffn(x, mod_l, w["norm_g"][l, 2], w["wg"][l, 1], w["wu"][l, 1], w["wd"][l, 1], seq=seq, sub=2,
                 mix=(of, ob, p["sg"], ona, w["hg_norm_g"][l], w["w_out"][l]))
    return x.reshape(nb, seq, D_MODEL)


def kernel(x_prompt, x_sample, c_prompt, c_sample, w_mod, b_mod, norm_g, ffn_w_gate, ffn_w_up, ffn_w_down,
           w_in, w_out, hg_lb_fwd, hg_lb_bwd, hg_norm_g, na_q_norm_g, na_k_norm_g, na_rpb):
    nbp = x_prompt.shape[0]
    c_all = jnp.concatenate([c_prompt, c_sample], axis=0)
    mod = _modulation(c_all, w_mod, b_mod).reshape(DEPTH, c_all.shape[0], 9, D_MODEL)
    per_head = lambda g, scale: jnp.tile(g.astype(F32) * scale, (1, NA_HEADS)).reshape(DEPTH, 1, NA_WIDTH)
    w = {
        "norm_g": norm_g.reshape(DEPTH, 3, 1, D_MODEL),
        "wg": ffn_w_gate.astype(BF16), "wu": ffn_w_up.astype(BF16), "wd": ffn_w_down.astype(BF16),
        "w_in": w_in.astype(BF16), "w_out": w_out.astype(BF16),
        "lb_f": _lower_bounds(hg_lb_fwd).reshape(DEPTH, 1, HG_WIDTH),
        "lb_b": _lower_bounds(hg_lb_bwd).reshape(DEPTH, 1, HG_WIDTH),
        "hg_norm_g": hg_norm_g.reshape(DEPTH, 1, LANES),
        "qg": per_head(na_q_norm_g, NA_DH ** -0.5 * LOG2E), "kg": per_head(na_k_norm_g, 1.0),
        "bias": _na_bias_tables(na_rpb),
    }
    consts = _hgrn_constants()
    y_prompt = _trunk(x_prompt, mod[:, :nbp], x_prompt.shape[1], w, consts)
    y_sample = _trunk(x_sample, mod[:, nbp:], x_sample.shape[1], w, consts)
    return (y_prompt, y_sample)
```

```python
import functools
import math

import numpy as np
import jax
import jax.numpy as jnp
from jax import lax
from jax.experimental import pallas as pl
from jax.experimental.pallas import tpu as pltpu

F32 = jnp.float32
BF16 = jnp.bfloat16

D_MODEL = 1024
DEPTH = 4
GRID_W = 64
HG_HEADS = 4
HG_DK = 128
HG_WIDTH = 512
NA_HEADS = 8
NA_DH = 64
NA_WIDTH = 512
WIN_ROWS = 8
WIN_COLS = 16
D_FF = 2816
IN_WIDTH = 4096
EPS = 1e-6
LOG2E = math.log2(math.e)

LANES = 128
SUBLANES = 8
TM = 512
HG_C = 128
HG_R = 1024
HG_LEVELS = tuple(HG_C >> i for i in range(HG_C.bit_length() - 1))
HG_COARSE = tuple(s for s in HG_LEVELS if s // 2 >= SUBLANES)
HG_FINE = tuple(s for s in HG_LEVELS if s // 2 < SUBLANES)
HG_GROUPS = 2 * (HG_HEADS // 2)
HG_CHAINS = 2 * HG_HEADS
HG_NX = len(HG_LEVELS) + 2
NA_ROWS_PER_ITER = 8
NEG_BIG = -1e30
VMEM_LIMIT = 56 * 1024 * 1024

NT_DIMS = (((1,), (1,)), ((), ()))
TN_DIMS = (((0,), (0,)), ((), ()))


def _dot(a, b):
    return jnp.dot(a, b, preferred_element_type=F32)


def _dot_nt(a, b):
    return lax.dot_general(a, b, NT_DIMS, preferred_element_type=F32)


def _sigmoid(x):
    return 1.0 / (1.0 + jnp.exp(-x))


def _silu(x):
    return x * _sigmoid(x)


def _layer_spec(shape, *lead):
    tail = (0,) * len(shape)
    return pl.BlockSpec((None,) * len(lead) + tuple(shape), lambda *_: tuple(lead) + tail,
                        pipeline_mode=pl.Buffered(1))


def _const_spec(shape):
    return _layer_spec(shape)


def _params(*sem):
    return pltpu.CompilerParams(dimension_semantics=sem, vmem_limit_bytes=VMEM_LIMIT)


def _mod_kernel(c_ref, w_ref, b_ref, o_ref):
    a = _silu(c_ref[...]).astype(BF16)
    o_ref[0] = _dot(a, w_ref[0].astype(BF16)) + b_ref[0]


def _modulation(c_all, w_mod, b_mod):
    nb = c_all.shape[0]
    tn = 1152
    return pl.pallas_call(
        _mod_kernel,
        grid=(DEPTH, 9 * D_MODEL // tn),
        in_specs=[
            pl.BlockSpec((nb, D_MODEL), lambda l, j: (0, 0)),
            pl.BlockSpec((1, D_MODEL, tn), lambda l, j: (l, 0, j)),
            pl.BlockSpec((1, 1, tn), lambda l, j: (l, 0, j)),
        ],
        out_specs=pl.BlockSpec((1, nb, tn), lambda l, j: (l, 0, j)),
        out_shape=jax.ShapeDtypeStruct((DEPTH, nb, 9 * D_MODEL), F32),
        compiler_params=_params("arbitrary", "arbitrary"),
        name="modulation",
    )(c_all, w_mod, b_mod.reshape(DEPTH, 1, 9 * D_MODEL))


def _norm_modulate(x, g, shift, scale):
    ms = jnp.mean(x * x, axis=-1, keepdims=True)
    h = (x * lax.rsqrt(ms + EPS)) * g
    return h * (1.0 + scale) + shift


def _mod_spec(l, b0, seq):
    return pl.BlockSpec((None, 1, 9, D_MODEL), lambda i: (l, b0 + i * TM // seq, 0, 0))


def _ffn_kernel(*refs, sub, with_mix):
    if with_mix:
        (x_ref, of_ref, ob_ref, sg_ref, ona_ref, hng_ref, wout_ref,
         mod_ref, ng_ref, wg_ref, wu_ref, wd_ref, o_ref) = refs
    else:
        x_ref, mod_ref, ng_ref, wg_ref, wu_ref, wd_ref, o_ref = refs
    x = x_ref[...]
    if with_mix:
        heads = []
        for h in range(HG_HEADS):
            cols = slice(h * LANES, (h + 1) * LANES)
            o = of_ref[:, cols] + ob_ref[:, cols]
            ms = jnp.mean(o * o, axis=-1, keepdims=True)
            o = (o * lax.rsqrt(ms + EPS)) * hng_ref[...]
            heads.append((o * sg_ref[:, cols].astype(F32)).astype(BF16))
        ohg = jnp.concatenate(heads, axis=-1)
        mix = _dot(ohg, wout_ref[0:HG_WIDTH, :]) + _dot(ona_ref[...], wout_ref[HG_WIDTH:, :])
        x = x + mod_ref[0, 5:6, :] * mix
    shift = mod_ref[0, 3 * sub:3 * sub + 1, :]
    scale = mod_ref[0, 3 * sub + 1:3 * sub + 2, :]
    gate = mod_ref[0, 3 * sub + 2:3 * sub + 3, :]
    h = _norm_modulate(x, ng_ref[...], shift, scale).astype(BF16)
    g = _dot(h, wg_ref[...])
    u = _dot(h, wu_ref[...])
    a = (_silu(g) * u).astype(BF16)
    y = _dot(a, wd_ref[...])
    o_ref[...] = x + 0.5 * gate * y


def _ffn(x, w, l, ffn_idx, *, seq, b0, mix=None):
    n = x.shape[0]
    sub = 2 * ffn_idx
    tok = lambda width: pl.BlockSpec((TM, width), lambda i: (i, 0))
    args, specs = [x], [tok(D_MODEL)]
    if mix is not None:
        of, ob, sg, ona = mix
        args += [of, ob, sg, ona, w["hg_norm_g"], w["w_out"]]
        specs += [tok(HG_WIDTH), tok(HG_WIDTH), tok(HG_WIDTH), tok(NA_WIDTH),
                  _layer_spec((1, LANES), l), _layer_spec((D_MODEL, D_MODEL), l)]
    args += [w["mod"], w["norm_g"], w["wg"], w["wu"], w["wd"]]
    specs += [_mod_spec(l, b0, seq), _layer_spec((1, D_MODEL), l, sub),
              _layer_spec((D_MODEL, D_FF), l, ffn_idx), _layer_spec((D_MODEL, D_FF), l, ffn_idx),
              _layer_spec((D_FF, D_MODEL), l, ffn_idx)]
    return pl.pallas_call(
        functools.partial(_ffn_kernel, sub=sub, with_mix=mix is not None),
        grid=(n // TM,),
        in_specs=specs,
        out_specs=tok(D_MODEL),
        out_shape=jax.ShapeDtypeStruct((n, D_MODEL), F32),
        compiler_params=_params("arbitrary"),
        name="ffn_mix" if mix is not None else "ffn",
    )(*args)


PROJ_OUTS = ("q", "kf", "kb", "gfh", "gfl", "gbh", "gbl", "v", "sg", "nq", "nk", "nv")


def _proj_kernel(x_ref, mod_ref, ng_ref, w_ref, lbf_ref, lbb_ref, qg_ref, kg_ref, *outs):
    o = dict(zip(PROJ_OUTS, outs))
    h = _norm_modulate(x_ref[...], ng_ref[...], mod_ref[0, 3:4, :], mod_ref[0, 4:5, :]).astype(BF16)
    seg = lambda i: _dot(h, w_ref[:, i * HG_WIDTH:(i + 1) * HG_WIDTH])

    def decay(i, lb_ref, k_ref, hi_ref, lo_ref):
        lb = lb_ref[...]
        f = lb + (1.0 - lb) * _sigmoid(seg(i))
        k_ref[...] = (1.0 - f).astype(BF16)
        g = jnp.log2(f)
        hi = g.astype(BF16)
        hi_ref[...] = hi
        lo_ref[...] = (g - hi.astype(F32)).astype(BF16)

    lane = lax.broadcasted_iota(jnp.int32, (TM, LANES), 1)
    first = lane < NA_DH

    def head_norm(i, g_ref, o_ref):
        x = seg(i)
        for t in range(NA_WIDTH // LANES):
            cols = slice(t * LANES, (t + 1) * LANES)
            xt = x[:, cols]
            x2 = xt * xt
            sa = jnp.sum(jnp.where(first, x2, 0.0), axis=-1, keepdims=True)
            sb = jnp.sum(jnp.where(first, 0.0, x2), axis=-1, keepdims=True)
            ms = jnp.where(first, sa, sb) * (1.0 / NA_DH)
            o_ref[:, cols] = ((xt * lax.rsqrt(ms + EPS)) * g_ref[:, cols]).astype(BF16)

    decay(1, lbf_ref, o["kf"], o["gfh"], o["gfl"])
    decay(2, lbb_ref, o["kb"], o["gbh"], o["gbl"])
    head_norm(5, qg_ref, o["nq"])
    head_norm(6, kg_ref, o["nk"])
    o["q"][...] = _silu(seg(0)).astype(BF16)
    o["sg"][...] = _silu(seg(4)).astype(BF16)
    o["v"][...] = seg(3).astype(BF16)
    o["nv"][...] = seg(7).astype(BF16)


def _in_proj(x, w, l, *, seq, b0):
    n = x.shape[0]
    vec = _layer_spec((1, HG_WIDTH), l)
    outs = pl.pallas_call(
        _proj_kernel,
        grid=(n // TM,),
        in_specs=[
            pl.BlockSpec((TM, D_MODEL), lambda i: (i, 0)),
            _mod_spec(l, b0, seq),
            _layer_spec((1, D_MODEL), l, 1),
            _layer_spec((D_MODEL, IN_WIDTH), l),
            vec, vec, vec, vec,
        ],
        out_specs=[pl.BlockSpec((TM, HG_WIDTH), lambda i: (i, 0))] * len(PROJ_OUTS),
        out_shape=[jax.ShapeDtypeStruct((n, HG_WIDTH), BF16)] * len(PROJ_OUTS),
        compiler_params=_params("arbitrary"),
        name="in_proj",
    )(x, w["mod"], w["norm_g"], w["w_in"], w["lb_f"], w["lb_b"], w["qg"], w["kg"])
    return dict(zip(PROJ_OUTS, outs))


def _hgrn_constants():
    c = HG_C
    idx = np.arange(c)
    t = idx[None, :]
    i = idx[:, None]
    run_blocks = [t <= i]
    sels, masks = [], []
    for s in HG_LEVELS:
        half = s // 2
        p = idx % s
        start = idx - p
        m = (start + half - 1)[:, None]
        is_q = p >= half
        masks.append(is_q[:, None] & ~is_q[None, :] & (start[:, None] == start[None, :]))
        if s in HG_FINE:
            qrow = (t >= m + 1) & (t <= i)
            krow = (t >= i + 1) & (t <= m)
            run_blocks.append(np.where(is_q[:, None], qrow, krow))
            sels.append(np.broadcast_to(is_q[:, None], (c, LANES)))
    masks.append(np.eye(c, dtype=bool))
    assert len(masks) % 2 == 0
    both = lambda a: np.stack([a, a[:, ::-1, ::-1]])
    run = both(np.stack(run_blocks).astype(np.float32))
    run = np.concatenate([run, run], axis=-1).reshape(2, -1, 2 * c)
    msk = both(np.stack(masks).astype(np.float32))
    msk = msk.reshape(2, len(masks) // 2, 2, c, c).transpose(0, 1, 3, 2, 4).reshape(2, -1, c, 2 * c)
    sel = np.stack(sels).astype(np.float32)
    sel = np.stack([sel, sel[:, ::-1, :]])
    return jnp.asarray(run, dtype=BF16), jnp.asarray(msk, dtype=F32), jnp.asarray(sel, dtype=F32)


def _hgrn_operands(d, b, a_fine, q, kk, selq, x_ref, e_ref):
    c = HG_C
    for l, s in enumerate(HG_COARSE):
        half = s // 2
        parts = []
        for blk in range(c // half):
            rows = slice(blk * half, (blk + 1) * half)
            base = (blk // 2) * s
            if d == 0:
                m, is_q = base + half - 1, blk % 2 == 1
            else:
                m, is_q = base + half, blk % 2 == 0
            bm = b[m:m + 1]
            if is_q:
                parts.append(q[rows] * jnp.exp2(b[rows] - bm))
            else:
                parts.append(kk[rows] * jnp.exp2(bm - b[rows]))
        x_ref[l] = jnp.concatenate(parts, axis=0).astype(BF16)
    for lf in range(len(HG_FINE)):
        e = jnp.exp2(a_fine[lf * c:(lf + 1) * c])
        x_ref[len(HG_COARSE) + lf] = (jnp.where(selq[lf], q, kk) * e).astype(BF16)
    last = c - 1 if d == 0 else 0
    b_tot = b[last:last + 1]
    x_ref[HG_NX - 2] = (q * jnp.exp2(b)).astype(BF16)
    x_ref[HG_NX - 1] = (kk * jnp.exp2(b_tot - b)).astype(BF16)
    e_ref[...] = jnp.broadcast_to(jnp.exp2(b_tot), e_ref.shape)


def _hgrn_kernel(qf_ref, kf_ref, gfh_ref, gfl_ref, vf_ref, qb_ref, kb_ref, gbh_ref, gbl_ref, vb_ref,
                 run_ref, msk_ref, sel_ref, of_ref, ob_ref, st_ref,
                 a0_ref, a1_ref, x0_ref, x1_ref, e0_ref, e1_ref):
    c = HG_C
    nchunk = HG_R // c
    assert nchunk % 2 == 0
    nlev = len(HG_LEVELS)
    npair = (nlev + 1) // 2
    a_refs, x_refs, e_refs = (a0_ref, a1_ref), (x0_ref, x1_ref), (e0_ref, e1_ref)

    @pl.when(pl.program_id(1) == 0)
    def _():
        st_ref[...] = jnp.zeros_like(st_ref)

    dirs = ((qf_ref, kf_ref, gfh_ref, gfl_ref, vf_ref, of_ref),
            (qb_ref, kb_ref, gbh_ref, gbl_ref, vb_ref, ob_ref))
    zero = jnp.zeros((c, LANES), BF16)

    def chunk_rows(j, d):
        j = jnp.minimum(j, nchunk - 1)
        jj = j if d == 0 else nchunk - 1 - j
        return pl.ds(pl.multiple_of(jj * c, c), c)

    def run_sums(j, slot):
        for d, (_, _, gh_ref, gl_ref, _, _) in enumerate(dirs):
            rows = chunk_rows(j, d)
            for hp in range(HG_HEADS // 2):
                cols2 = slice(hp * 2 * LANES, (hp + 1) * 2 * LANES)
                g2 = jnp.concatenate([gh_ref[rows, cols2], gl_ref[rows, cols2]], axis=0)
                a_refs[slot][2 * d + hp] = _dot(run_ref[d], g2)

    def operands(j, slot, d, h, selq):
        q_ref, k_ref = dirs[d][0], dirs[d][1]
        rows = chunk_rows(j, d)
        cols = slice(h * LANES, (h + 1) * LANES)
        lanes = slice((h % 2) * LANES, (h % 2 + 1) * LANES)
        a_ref = a_refs[slot]
        ci = d * HG_HEADS + h
        _hgrn_operands(d, a_ref[2 * d + h // 2, 0:c, lanes], a_ref[2 * d + h // 2, c:, lanes],
                       q_ref[rows, cols].astype(F32), k_ref[rows, cols].astype(F32), selq,
                       x_refs[slot].at[ci], e_refs[slot].at[ci])

    def level_pair(j, slot, d, h, i):
        q_ref, k_ref = dirs[d][0], dirs[d][1]
        rows = chunk_rows(j, d)
        cols = slice(h * LANES, (h + 1) * LANES)
        x_ref = x_refs[slot].at[d * HG_HEADS + h]
        l1 = r1 = x_ref[2 * i]
        if 2 * i + 1 < nlev:
            l2 = r2 = x_ref[2 * i + 1]
        else:
            l2, r2 = q_ref[rows, cols], k_ref[rows, cols]
        g2 = _dot_nt(jnp.concatenate([l1, l2], axis=1),
                     jnp.concatenate([jnp.concatenate([r1, zero], axis=1),
                                      jnp.concatenate([zero, r2], axis=1)], axis=0)) * msk_ref[d, i]
        return g2[:, :c] + g2[:, c:]

    def finish(j, slot, d, h, p):
        v_ref, o_ref = dirs[d][4], dirs[d][5]
        rows = chunk_rows(j, d)
        cols = slice(h * LANES, (h + 1) * LANES)
        ci = d * HG_HEADS + h
        x_ref = x_refs[slot].at[ci]
        vb = v_ref[rows, cols]
        st = st_ref[d, h]
        o_ref[rows, cols] = _dot(jnp.concatenate([p.astype(BF16), x_ref[HG_NX - 2]], axis=1),
                                 jnp.concatenate([vb, st.astype(BF16)], axis=0))
        e_col = jnp.transpose(jnp.broadcast_to(e_refs[slot][ci, 0:1, :], (c, LANES)))
        st_ref[d, h] = st * e_col + lax.dot_general(x_ref[HG_NX - 1], vb, TN_DIMS, preferred_element_type=F32)

    def sel_masks():
        return [[sel_ref[d, lf] != 0.0 for lf in range(len(HG_FINE))] for d in range(2)]

    chains = [(d, h) for d in range(2) for h in range(HG_HEADS)]
    assert len(chains) % npair == 0

    def step(j, slot):
        run_sums(j + 2, slot)
        selq = sel_masks()
        ps = [None] * len(chains)
        per = len(chains) // npair
        for i in range(npair):
            for ci, (d, h) in enumerate(chains):
                g = level_pair(j, slot, d, h, i)
                ps[ci] = g if ps[ci] is None else ps[ci] + g
            for d, h in chains[i * per:(i + 1) * per]:
                operands(j + 1, 1 - slot, d, h, selq[d])
        for ci, (d, h) in enumerate(chains):
            finish(j, slot, d, h, ps[ci])

    def body(i, carry):
        step(2 * i, 0)
        step(2 * i + 1, 1)
        return carry

    run_sums(0, 0)
    run_sums(1, 1)
    selq = sel_masks()
    for d in range(2):
        for h in range(HG_HEADS):
            operands(0, 0, d, h, selq[d])
    lax.fori_loop(0, nchunk // 2, body, 0)


def _hgrn(p, consts, *, seq):
    n = p["q"].shape[0]
    nb, nt = n // seq, seq // HG_R
    run, msk, sel = consts
    fwd = pl.BlockSpec((HG_R, HG_WIDTH), lambda b, t: (b * nt + t, 0))
    bwd = pl.BlockSpec((HG_R, HG_WIDTH), lambda b, t: (b * nt + nt - 1 - t, 0))
    out = jax.ShapeDtypeStruct((n, HG_WIDTH), F32)
    return pl.pallas_call(
        _hgrn_kernel,
        grid=(nb, nt),
        in_specs=[fwd] * 5 + [bwd] * 5 + [_const_spec(run.shape), _const_spec(msk.shape), _const_spec(sel.shape)],
        out_specs=[fwd, bwd],
        out_shape=[out, out],
        scratch_shapes=[pltpu.VMEM((2, HG_HEADS, HG_DK, LANES), F32)]
        + [pltpu.VMEM((HG_GROUPS, run.shape[1], 2 * LANES), F32)] * 2
        + [pltpu.VMEM((HG_CHAINS, HG_NX, HG_C, LANES), BF16)] * 2
        + [pltpu.VMEM((HG_CHAINS, SUBLANES, LANES), F32)] * 2,
        compiler_params=_params("arbitrary", "arbitrary"),
        name="hgrn2",
    )(p["q"], p["kf"], p["gfh"], p["gfl"], p["v"], p["q"], p["kb"], p["gbh"], p["gbl"], p["v"], run, msk, sel)


def _na_kernel(q_ref, k_ref, v_ref, bias_ref, o_ref, *, seq):
    rows = seq // GRID_W
    win = WIN_ROWS * GRID_W
    lane = lax.broadcasted_iota(jnp.int32, (GRID_W, LANES), 1)
    first = lane < NA_DH

    def body(i, carry):
        rs = [i * NA_ROWS_PER_ITER + u for u in range(NA_ROWS_PER_ITER)]
        r0s = [jnp.clip(r - WIN_ROWS // 2, 0, rows - WIN_ROWS) for r in rs]
        qrows = [pl.ds(pl.multiple_of(r * GRID_W, GRID_W), GRID_W) for r in rs]
        wrows = [pl.ds(pl.multiple_of(r0 * GRID_W, GRID_W), win) for r0 in r0s]
        ss = []
        for r, r0, qr_, wr_ in zip(rs, r0s, qrows, wrows):
            qr = q_ref[qr_, :]
            zero = jnp.zeros_like(qr)
            q2 = jnp.concatenate([jnp.where(first, qr, zero), jnp.where(first, zero, qr)], axis=0)
            ss.append(_dot_nt(q2, k_ref[wr_, :]) + bias_ref[0, r - r0])
        ps, ls = [], []
        for s in ss:
            p = jnp.exp2(s - jnp.max(s, axis=-1, keepdims=True))
            ls.append(jnp.sum(p, axis=-1, keepdims=True))
            ps.append(p.astype(BF16))
        for p, l, qr_, wr_ in zip(ps, ls, qrows, wrows):
            o2 = _dot(p, v_ref[wr_, :]) * (1.0 / l)
            o_ref[qr_, :] = jnp.where(first, o2[:GRID_W], o2[GRID_W:]).astype(BF16)
        return carry

    lax.fori_loop(0, rows // NA_ROWS_PER_ITER, body, 0)


def _na_bias_tables(na_rpb):
    qcol = np.arange(GRID_W)
    c0 = np.clip(qcol - WIN_COLS // 2, 0, GRID_W - WIN_COLS)
    col_mask = (qcol[None, :] >= c0[:, None]) & (qcol[None, :] < c0[:, None] + WIN_COLS)
    rel_c = np.clip(qcol[None, :] - qcol[:, None], -(WIN_COLS - 1), WIN_COLS - 1) + WIN_COLS - 1
    onehot = (rel_c[:, :, None] == np.arange(2 * WIN_COLS - 1)).astype(np.float32)
    t = jnp.einsum("qkc,lhrc->lhrqk", onehot, na_rpb.astype(F32), precision=lax.Precision.HIGHEST)
    t = jnp.where(col_mask, t * LOG2E, NEG_BIG)
    per_off = [t[:, :, WIN_ROWS - 1 - dl:2 * WIN_ROWS - 1 - dl] for dl in range(WIN_ROWS)]
    b = jnp.stack(per_off, axis=2)
    b = b.transpose(0, 1, 2, 4, 3, 5).reshape(DEPTH, NA_HEADS // 2, 2, WIN_ROWS, GRID_W, WIN_ROWS * GRID_W)
    return b.transpose(0, 1, 3, 2, 4, 5).reshape(DEPTH, NA_HEADS // 2, WIN_ROWS, 2 * GRID_W, WIN_ROWS * GRID_W)


def _na(p, w, l, *, seq):
    n = p["nq"].shape[0]
    nb = n // seq
    col = pl.BlockSpec((seq, LANES), lambda hp, b: (b, hp))
    return pl.pallas_call(
        functools.partial(_na_kernel, seq=seq),
        grid=(NA_HEADS // 2, nb),
        in_specs=[col, col, col,
                  pl.BlockSpec((None, 1, WIN_ROWS, 2 * GRID_W, WIN_ROWS * GRID_W),
                               lambda hp, b: (l, hp, 0, 0, 0))],
        out_specs=col,
        out_shape=jax.ShapeDtypeStruct((n, NA_WIDTH), BF16),
        compiler_params=_params("arbitrary", "arbitrary"),
        name="natten",
    )(p["nq"], p["nk"], p["nv"], w["bias"])


def _lower_bounds(p):
    sm = jax.nn.softmax(p.astype(F32), axis=0)
    return jnp.cumsum(sm, axis=0) - sm[0]


def _trunk(x, b0, w, consts):
    nb, seq, _ = x.shape
    x = x.reshape(nb * seq, D_MODEL)
    for l in range(DEPTH):
        x = _ffn(x, w, l, 0, seq=seq, b0=b0)
        p = _in_proj(x, w, l, seq=seq, b0=b0)
        of, ob = _hgrn(p, consts, seq=seq)
        ona = _na(p, w, l, seq=seq)
        x = _ffn(x, w, l, 1, seq=seq, b0=b0, mix=(of, ob, p["sg"], ona))
    return x.reshape(nb, seq, D_MODEL)


def kernel(x_prompt, x_sample, c_prompt, c_sample, w_mod, b_mod, norm_g, ffn_w_gate, ffn_w_up, ffn_w_down,
           w_in, w_out, hg_lb_fwd, hg_lb_bwd, hg_norm_g, na_q_norm_g, na_k_norm_g, na_rpb):
    nbp = x_prompt.shape[0]
    c_all = jnp.concatenate([c_prompt, c_sample], axis=0)
    per_head = lambda g, scale: jnp.tile(g.astype(F32) * scale, (1, NA_HEADS)).reshape(DEPTH, 1, NA_WIDTH)
    w = {
        "mod": _modulation(c_all, w_mod, b_mod).reshape(DEPTH, c_all.shape[0], 9, D_MODEL),
        "norm_g": norm_g.reshape(DEPTH, 3, 1, D_MODEL),
        "wg": ffn_w_gate.astype(BF16), "wu": ffn_w_up.astype(BF16), "wd": ffn_w_down.astype(BF16),
        "w_in": w_in.astype(BF16), "w_out": w_out.astype(BF16),
        "lb_f": _lower_bounds(hg_lb_fwd).reshape(DEPTH, 1, HG_WIDTH),
        "lb_b": _lower_bounds(hg_lb_bwd).reshape(DEPTH, 1, HG_WIDTH),
        "hg_norm_g": hg_norm_g.reshape(DEPTH, 1, LANES),
        "qg": per_head(na_q_norm_g, NA_DH ** -0.5 * LOG2E), "kg": per_head(na_k_norm_g, 1.0),
        "bias": _na_bias_tables(na_rpb),
    }
    consts = _hgrn_constants()
    y_prompt = _trunk(x_prompt, 0, w, consts)
    y_sample = _trunk(x_sample, nbp, w, consts)
    return (y_prompt, y_sample)
```

```python
import functools
import math

import numpy as np
import jax
import jax.numpy as jnp
from jax import lax
from jax.experimental import pallas as pl
from jax.experimental.pallas import tpu as pltpu

F32 = jnp.float32
BF16 = jnp.bfloat16

D_MODEL = 1024
DEPTH = 4
GRID_W = 64
HG_HEADS = 4
HG_DK = 128
HG_WIDTH = 512
NA_HEADS = 8
NA_DH = 64
NA_WIDTH = 512
WIN_ROWS = 8
WIN_COLS = 16
D_FF = 2816
IN_WIDTH = 4096
EPS = 1e-6
LOG2E = math.log2(math.e)

LANES = 128
SUBLANES = 8
TM = 512
HG_C = 128
HG_R = 1024
HG_LEVELS = tuple(HG_C >> i for i in range(HG_C.bit_length() - 1))
HG_COARSE = tuple(s for s in HG_LEVELS if s // 2 >= SUBLANES)
HG_FINE = tuple(s for s in HG_LEVELS if s // 2 < SUBLANES)
HG_GROUPS = 2 * (HG_HEADS // 2)
HG_CHAINS = 2 * HG_HEADS
NA_ROWS_PER_ITER = 8
NEG_BIG = -1e30
VMEM_LIMIT = 56 * 1024 * 1024

NT_DIMS = (((1,), (1,)), ((), ()))
TN_DIMS = (((0,), (0,)), ((), ()))


def _dot(a, b):
    return jnp.dot(a, b, preferred_element_type=F32)


def _dot_nt(a, b):
    return lax.dot_general(a, b, NT_DIMS, preferred_element_type=F32)


def _sigmoid(x):
    return 1.0 / (1.0 + jnp.exp(-x))


def _silu(x):
    return x * _sigmoid(x)


def _layer_spec(shape, *lead):
    tail = (0,) * len(shape)
    return pl.BlockSpec((None,) * len(lead) + tuple(shape), lambda *_: tuple(lead) + tail,
                        pipeline_mode=pl.Buffered(1))


def _const_spec(shape):
    return _layer_spec(shape)


def _params(*sem):
    return pltpu.CompilerParams(dimension_semantics=sem, vmem_limit_bytes=VMEM_LIMIT)


def _mod_kernel(c_ref, w_ref, b_ref, o_ref):
    a = _silu(c_ref[...]).astype(BF16)
    o_ref[0] = _dot(a, w_ref[0].astype(BF16)) + b_ref[0]


def _modulation(c_all, w_mod, b_mod):
    nb = c_all.shape[0]
    tn = 1152
    return pl.pallas_call(
        _mod_kernel,
        grid=(DEPTH, 9 * D_MODEL // tn),
        in_specs=[
            pl.BlockSpec((nb, D_MODEL), lambda l, j: (0, 0)),
            pl.BlockSpec((1, D_MODEL, tn), lambda l, j: (l, 0, j)),
            pl.BlockSpec((1, 1, tn), lambda l, j: (l, 0, j)),
        ],
        out_specs=pl.BlockSpec((1, nb, tn), lambda l, j: (l, 0, j)),
        out_shape=jax.ShapeDtypeStruct((DEPTH, nb, 9 * D_MODEL), F32),
        compiler_params=_params("arbitrary", "arbitrary"),
        name="modulation",
    )(c_all, w_mod, b_mod.reshape(DEPTH, 1, 9 * D_MODEL))


def _norm_modulate(x, g, shift, scale):
    ms = jnp.mean(x * x, axis=-1, keepdims=True)
    h = (x * lax.rsqrt(ms + EPS)) * g
    return h * (1.0 + scale) + shift


def _mod_spec(l, b0, seq):
    return pl.BlockSpec((None, 1, 9, D_MODEL), lambda i: (l, b0 + i * TM // seq, 0, 0))


def _ffn_kernel(*refs, sub, with_mix):
    if with_mix:
        (x_ref, of_ref, ob_ref, sg_ref, ona_ref, hng_ref, wout_ref,
         mod_ref, ng_ref, wg_ref, wu_ref, wd_ref, o_ref) = refs
    else:
        x_ref, mod_ref, ng_ref, wg_ref, wu_ref, wd_ref, o_ref = refs
    x = x_ref[...]
    if with_mix:
        heads = []
        for h in range(HG_HEADS):
            cols = slice(h * LANES, (h + 1) * LANES)
            o = of_ref[:, cols] + ob_ref[:, cols]
            ms = jnp.mean(o * o, axis=-1, keepdims=True)
            o = (o * lax.rsqrt(ms + EPS)) * hng_ref[...]
            heads.append((o * sg_ref[:, cols].astype(F32)).astype(BF16))
        ohg = jnp.concatenate(heads, axis=-1)
        mix = _dot(ohg, wout_ref[0:HG_WIDTH, :]) + _dot(ona_ref[...], wout_ref[HG_WIDTH:, :])
        x = x + mod_ref[0, 5:6, :] * mix
    shift = mod_ref[0, 3 * sub:3 * sub + 1, :]
    scale = mod_ref[0, 3 * sub + 1:3 * sub + 2, :]
    gate = mod_ref[0, 3 * sub + 2:3 * sub + 3, :]
    h = _norm_modulate(x, ng_ref[...], shift, scale).astype(BF16)
    g = _dot(h, wg_ref[...])
    u = _dot(h, wu_ref[...])
    a = (_silu(g) * u).astype(BF16)
    y = _dot(a, wd_ref[...])
    o_ref[...] = x + 0.5 * gate * y


def _ffn(x, w, l, ffn_idx, *, seq, b0, mix=None):
    n = x.shape[0]
    sub = 2 * ffn_idx
    tok = lambda width: pl.BlockSpec((TM, width), lambda i: (i, 0))
    args, specs = [x], [tok(D_MODEL)]
    if mix is not None:
        of, ob, sg, ona = mix
        args += [of, ob, sg, ona, w["hg_norm_g"], w["w_out"]]
        specs += [tok(HG_WIDTH), tok(HG_WIDTH), tok(HG_WIDTH), tok(NA_WIDTH),
                  _layer_spec((1, LANES), l), _layer_spec((D_MODEL, D_MODEL), l)]
    args += [w["mod"], w["norm_g"], w["wg"], w["wu"], w["wd"]]
    specs += [_mod_spec(l, b0, seq), _layer_spec((1, D_MODEL), l, sub),
              _layer_spec((D_MODEL, D_FF), l, ffn_idx), _layer_spec((D_MODEL, D_FF), l, ffn_idx),
              _layer_spec((D_FF, D_MODEL), l, ffn_idx)]
    return pl.pallas_call(
        functools.partial(_ffn_kernel, sub=sub, with_mix=mix is not None),
        grid=(n // TM,),
        in_specs=specs,
        out_specs=tok(D_MODEL),
        out_shape=jax.ShapeDtypeStruct((n, D_MODEL), F32),
        compiler_params=_params("arbitrary"),
        name="ffn_mix" if mix is not None else "ffn",
    )(*args)


PROJ_OUTS = ("q", "kf", "kb", "gfh", "gfl", "gbh", "gbl", "v", "sg", "nq", "nk", "nv")


def _proj_kernel(x_ref, mod_ref, ng_ref, w_ref, lbf_ref, lbb_ref, qg_ref, kg_ref, *outs):
    o = dict(zip(PROJ_OUTS, outs))
    h = _norm_modulate(x_ref[...], ng_ref[...], mod_ref[0, 3:4, :], mod_ref[0, 4:5, :]).astype(BF16)
    half_w = HG_WIDTH // 2
    sub = lambda hf: slice(hf * half_w, (hf + 1) * half_w)
    seg = lambda i, hf: _dot(h, w_ref[:, i * HG_WIDTH + hf * half_w:i * HG_WIDTH + (hf + 1) * half_w])

    def decay(i, hf, lb_ref, k_ref, hi_ref, lo_ref):
        lb = lb_ref[:, sub(hf)]
        f = lb + (1.0 - lb) * _sigmoid(seg(i, hf))
        k_ref[:, sub(hf)] = (1.0 - f).astype(BF16)
        g = jnp.log2(f)
        hi = g.astype(BF16)
        hi_ref[:, sub(hf)] = hi
        lo_ref[:, sub(hf)] = (g - hi.astype(F32)).astype(BF16)

    lane = lax.broadcasted_iota(jnp.int32, (TM, LANES), 1)
    first = lane < NA_DH

    def head_norm(i, hf, g_ref, o_ref):
        x = seg(i, hf)
        for t in range(half_w // LANES):
            cols = slice(hf * half_w + t * LANES, hf * half_w + (t + 1) * LANES)
            xt = x[:, t * LANES:(t + 1) * LANES]
            x2 = xt * xt
            sa = jnp.sum(jnp.where(first, x2, 0.0), axis=-1, keepdims=True)
            sb = jnp.sum(jnp.where(first, 0.0, x2), axis=-1, keepdims=True)
            ms = jnp.where(first, sa, sb) * (1.0 / NA_DH)
            o_ref[:, cols] = ((xt * lax.rsqrt(ms + EPS)) * g_ref[:, cols]).astype(BF16)

    def plain(i, hf, o_ref, fn=lambda y: y):
        o_ref[:, sub(hf)] = fn(seg(i, hf)).astype(BF16)

    for hf in range(2):
        decay(1, hf, lbf_ref, o["kf"], o["gfh"], o["gfl"])
        plain(3, hf, o["v"])
    for hf in range(2):
        decay(2, hf, lbb_ref, o["kb"], o["gbh"], o["gbl"])
        plain(7, hf, o["nv"])
    for hf in range(2):
        head_norm(5, hf, qg_ref, o["nq"])
        plain(0, hf, o["q"], _silu)
    for hf in range(2):
        head_norm(6, hf, kg_ref, o["nk"])
        plain(4, hf, o["sg"], _silu)


def _in_proj(x, w, l, *, seq, b0):
    n = x.shape[0]
    vec = _layer_spec((1, HG_WIDTH), l)
    outs = pl.pallas_call(
        _proj_kernel,
        grid=(n // TM,),
        in_specs=[
            pl.BlockSpec((TM, D_MODEL), lambda i: (i, 0)),
            _mod_spec(l, b0, seq),
            _layer_spec((1, D_MODEL), l, 1),
            _layer_spec((D_MODEL, IN_WIDTH), l),
            vec, vec, vec, vec,
        ],
        out_specs=[pl.BlockSpec((TM, HG_WIDTH), lambda i: (i, 0))] * len(PROJ_OUTS),
        out_shape=[jax.ShapeDtypeStruct((n, HG_WIDTH), BF16)] * len(PROJ_OUTS),
        compiler_params=_params("arbitrary"),
        name="in_proj",
    )(x, w["mod"], w["norm_g"], w["w_in"], w["lb_f"], w["lb_b"], w["qg"], w["kg"])
    return dict(zip(PROJ_OUTS, outs))


def _hgrn_constants():
    c = HG_C
    idx = np.arange(c)
    t = idx[None, :]
    i = idx[:, None]
    run_blocks = [t <= i]
    sels, masks = [], []
    for s in HG_LEVELS:
        half = s // 2
        p = idx % s
        start = idx - p
        m = (start + half - 1)[:, None]
        is_q = p >= half
        masks.append(is_q[:, None] & ~is_q[None, :] & (start[:, None] == start[None, :]))
        if s in HG_FINE:
            qrow = (t >= m + 1) & (t <= i)
            krow = (t >= i + 1) & (t <= m)
            run_blocks.append(np.where(is_q[:, None], qrow, krow))
            sels.append(np.broadcast_to(is_q[:, None], (c, LANES)))
    masks.append(np.eye(c, dtype=bool))
    assert len(masks) % 2 == 0
    both = lambda a: np.stack([a, a[:, ::-1, ::-1]])
    run = both(np.stack(run_blocks).astype(np.float32))
    run = np.concatenate([run, run], axis=-1).reshape(2, -1, 2 * c)
    msk = both(np.stack(masks).astype(np.float32))
    msk = msk.reshape(2, len(masks) // 2, 2, c, c).transpose(0, 1, 3, 2, 4).reshape(2, -1, c, 2 * c)
    sel = np.stack(sels).astype(np.float32)
    sel = np.stack([sel, sel[:, ::-1, :]])
    return jnp.asarray(run, dtype=BF16), jnp.asarray(msk, dtype=BF16), jnp.asarray(sel, dtype=F32)


def _hgrn_operands(d, b, a_fine, q, kk, kkb, selq, x_ref, xt_ref, e_ref):
    c = HG_C
    nlev = len(HG_LEVELS)

    def put(l, x):
        xb = x.astype(BF16)
        x_ref[l] = xb
        xt_ref[l] = xb.T

    for l, s in enumerate(HG_COARSE):
        half = s // 2
        parts = []
        for blk in range(c // half):
            rows = slice(blk * half, (blk + 1) * half)
            base = (blk // 2) * s
            if d == 0:
                m, is_q = base + half - 1, blk % 2 == 1
            else:
                m, is_q = base + half, blk % 2 == 0
            bm = b[m:m + 1]
            if is_q:
                parts.append(q[rows] * jnp.exp2(b[rows] - bm))
            else:
                parts.append(kk[rows] * jnp.exp2(bm - b[rows]))
        put(l, jnp.concatenate(parts, axis=0))
    for lf in range(len(HG_FINE)):
        e = jnp.exp2(a_fine[lf * c:(lf + 1) * c])
        put(len(HG_COARSE) + lf, jnp.where(selq[lf], q, kk) * e)
    last = c - 1 if d == 0 else 0
    b_tot = b[last:last + 1]
    x_ref[nlev] = (q * jnp.exp2(b)).astype(BF16)
    xt_ref[nlev] = kkb.T
    xt_ref[nlev + 1] = (kk * jnp.exp2(b_tot - b)).astype(BF16).T
    e_ref[...] = jnp.broadcast_to(jnp.exp2(b_tot), e_ref.shape)


def _hgrn_kernel(qf_ref, kf_ref, gfh_ref, gfl_ref, vf_ref, qb_ref, kb_ref, gbh_ref, gbl_ref, vb_ref,
                 run_ref, msk_ref, sel_ref, of_ref, ob_ref, st_ref,
                 a0_ref, a1_ref, x0_ref, x1_ref, xt0_ref, xt1_ref, e0_ref, e1_ref):
    c = HG_C
    nchunk = HG_R // c
    assert nchunk % 2 == 0
    nlev = len(HG_LEVELS)
    npair = (nlev + 1) // 2
    a_refs, x_refs, e_refs = (a0_ref, a1_ref), (x0_ref, x1_ref), (e0_ref, e1_ref)
    xt_refs = (xt0_ref, xt1_ref)

    @pl.when(pl.program_id(1) == 0)
    def _():
        st_ref[...] = jnp.zeros_like(st_ref)

    dirs = ((qf_ref, kf_ref, gfh_ref, gfl_ref, vf_ref, of_ref),
            (qb_ref, kb_ref, gbh_ref, gbl_ref, vb_ref, ob_ref))
    zero = jnp.zeros((c, LANES), BF16)

    def chunk_rows(j, d):
        j = jnp.minimum(j, nchunk - 1)
        jj = j if d == 0 else nchunk - 1 - j
        return pl.ds(pl.multiple_of(jj * c, c), c)

    def run_sums(j, slot):
        for d, (_, _, gh_ref, gl_ref, _, _) in enumerate(dirs):
            rows = chunk_rows(j, d)
            for hp in range(HG_HEADS // 2):
                cols2 = slice(hp * 2 * LANES, (hp + 1) * 2 * LANES)
                g2 = jnp.concatenate([gh_ref[rows, cols2], gl_ref[rows, cols2]], axis=0)
                a_refs[slot][2 * d + hp] = _dot(run_ref[d], g2)

    def operands(j, slot, d, h, selq):
        q_ref, k_ref = dirs[d][0], dirs[d][1]
        rows = chunk_rows(j, d)
        cols = slice(h * LANES, (h + 1) * LANES)
        lanes = slice((h % 2) * LANES, (h % 2 + 1) * LANES)
        a_ref = a_refs[slot]
        ci = d * HG_HEADS + h
        kkb = k_ref[rows, cols]
        _hgrn_operands(d, a_ref[2 * d + h // 2, 0:c, lanes], a_ref[2 * d + h // 2, c:, lanes],
                       q_ref[rows, cols].astype(F32), kkb.astype(F32), kkb, selq,
                       x_refs[slot].at[ci], xt_refs[slot].at[ci], e_refs[slot].at[ci])

    def level_pair(j, slot, d, h, i):
        q_ref = dirs[d][0]
        rows = chunk_rows(j, d)
        cols = slice(h * LANES, (h + 1) * LANES)
        ci = d * HG_HEADS + h
        x_ref, xt_ref = x_refs[slot].at[ci], xt_refs[slot].at[ci]
        l2 = x_ref[2 * i + 1] if 2 * i + 1 < nlev else q_ref[rows, cols]
        g2 = _dot(jnp.concatenate([x_ref[2 * i], l2], axis=1),
                  jnp.concatenate([jnp.concatenate([xt_ref[2 * i], zero], axis=1),
                                   jnp.concatenate([zero, xt_ref[2 * i + 1]], axis=1)], axis=0))
        g2 = g2.astype(BF16) * msk_ref[d, i]
        return g2[:, :c] + g2[:, c:]

    def finish(j, slot, d, h, p):
        v_ref, o_ref = dirs[d][4], dirs[d][5]
        rows = chunk_rows(j, d)
        cols = slice(h * LANES, (h + 1) * LANES)
        ci = d * HG_HEADS + h
        vb = v_ref[rows, cols]
        st = st_ref[d, h]
        o_ref[rows, cols] = _dot(jnp.concatenate([p, x_refs[slot][ci, nlev]], axis=1),
                                 jnp.concatenate([vb, st.astype(BF16)], axis=0))
        e_col = jnp.transpose(jnp.broadcast_to(e_refs[slot][ci, 0:1, :], (c, LANES)))
        st_ref[d, h] = st * e_col + _dot(xt_refs[slot][ci, nlev + 1], vb)

    def sel_masks():
        return [[sel_ref[d, lf] != 0.0 for lf in range(len(HG_FINE))] for d in range(2)]

    chains = [(d, h) for d in range(2) for h in range(HG_HEADS)]
    assert len(chains) % npair == 0

    def step(j, slot):
        run_sums(j + 2, slot)
        selq = sel_masks()
        ps = [None] * len(chains)
        per = len(chains) // npair
        for i in range(npair):
            for ci, (d, h) in enumerate(chains):
                g = level_pair(j, slot, d, h, i)
                ps[ci] = g if ps[ci] is None else ps[ci] + g
            for d, h in chains[i * per:(i + 1) * per]:
                operands(j + 1, 1 - slot, d, h, selq[d])
        for ci, (d, h) in enumerate(chains):
            finish(j, slot, d, h, ps[ci])

    def body(i, carry):
        step(2 * i, 0)
        step(2 * i + 1, 1)
        return carry

    run_sums(0, 0)
    run_sums(1, 1)
    selq = sel_masks()
    for d in range(2):
        for h in range(HG_HEADS):
            operands(0, 0, d, h, selq[d])
    lax.fori_loop(0, nchunk // 2, body, 0)


def _hgrn(p, consts, *, seq):
    n = p["q"].shape[0]
    nb, nt = n // seq, seq // HG_R
    run, msk, sel = consts
    fwd = pl.BlockSpec((HG_R, HG_WIDTH), lambda b, t: (b * nt + t, 0))
    bwd = pl.BlockSpec((HG_R, HG_WIDTH), lambda b, t: (b * nt + nt - 1 - t, 0))
    out = jax.ShapeDtypeStruct((n, HG_WIDTH), F32)
    return pl.pallas_call(
        _hgrn_kernel,
        grid=(nb, nt),
        in_specs=[fwd] * 5 + [bwd] * 5 + [_const_spec(run.shape), _const_spec(msk.shape), _const_spec(sel.shape)],
        out_specs=[fwd, bwd],
        out_shape=[out, out],
        scratch_shapes=[pltpu.VMEM((2, HG_HEADS, HG_DK, LANES), F32)]
        + [pltpu.VMEM((HG_GROUPS, run.shape[1], 2 * LANES), F32)] * 2
        + [pltpu.VMEM((HG_CHAINS, len(HG_LEVELS) + 1, HG_C, LANES), BF16)] * 2
        + [pltpu.VMEM((HG_CHAINS, len(HG_LEVELS) + 2, HG_DK, HG_C), BF16)] * 2
        + [pltpu.VMEM((HG_CHAINS, SUBLANES, LANES), F32)] * 2,
        compiler_params=_params("arbitrary", "arbitrary"),
        name="hgrn2",
    )(p["q"], p["kf"], p["gfh"], p["gfl"], p["v"], p["q"], p["kb"], p["gbh"], p["gbl"], p["v"], run, msk, sel)


def _na_kernel(q_ref, k_ref, v_ref, bias_ref, o_ref, *, seq):
    rows = seq // GRID_W
    win = WIN_ROWS * GRID_W
    lane = lax.broadcasted_iota(jnp.int32, (GRID_W, LANES), 1)
    first = lane < NA_DH

    def body(i, carry):
        rs = [i * NA_ROWS_PER_ITER + u for u in range(NA_ROWS_PER_ITER)]
        r0s = [jnp.clip(r - WIN_ROWS // 2, 0, rows - WIN_ROWS) for r in rs]
        qrows = [pl.ds(pl.multiple_of(r * GRID_W, GRID_W), GRID_W) for r in rs]
        wrows = [pl.ds(pl.multiple_of(r0 * GRID_W, GRID_W), win) for r0 in r0s]
        ss = []
        for r, r0, qr_, wr_ in zip(rs, r0s, qrows, wrows):
            qr = q_ref[qr_, :]
            zero = jnp.zeros_like(qr)
            q2 = jnp.concatenate([jnp.where(first, qr, zero), jnp.where(first, zero, qr)], axis=0)
            ss.append(_dot_nt(q2, k_ref[wr_, :]) + bias_ref[0, r - r0])
        ps, ls = [], []
        for s in ss:
            p = jnp.exp2(s - jnp.max(s, axis=-1, keepdims=True))
            ls.append(jnp.sum(p, axis=-1, keepdims=True))
            ps.append(p.astype(BF16))
        for p, l, qr_, wr_ in zip(ps, ls, qrows, wrows):
            o2 = _dot(p, v_ref[wr_, :]) * (1.0 / l)
            o_ref[qr_, :] = jnp.where(first, o2[:GRID_W], o2[GRID_W:]).astype(BF16)
        return carry

    lax.fori_loop(0, rows // NA_ROWS_PER_ITER, body, 0)


def _na_bias_tables(na_rpb):
    qcol = np.arange(GRID_W)
    c0 = np.clip(qcol - WIN_COLS // 2, 0, GRID_W - WIN_COLS)
    col_mask = (qcol[None, :] >= c0[:, None]) & (qcol[None, :] < c0[:, None] + WIN_COLS)
    rel_c = np.clip(qcol[None, :] - qcol[:, None], -(WIN_COLS - 1), WIN_COLS - 1) + WIN_COLS - 1
    onehot = (rel_c[:, :, None] == np.arange(2 * WIN_COLS - 1)).astype(np.float32)
    t = jnp.einsum("qkc,lhrc->lhrqk", onehot, na_rpb.astype(F32), precision=lax.Precision.HIGHEST)
    t = jnp.where(col_mask, t * LOG2E, NEG_BIG)
    per_off = [t[:, :, WIN_ROWS - 1 - dl:2 * WIN_ROWS - 1 - dl] for dl in range(WIN_ROWS)]
    b = jnp.stack(per_off, axis=2)
    b = b.transpose(0, 1, 2, 4, 3, 5).reshape(DEPTH, NA_HEADS // 2, 2, WIN_ROWS, GRID_W, WIN_ROWS * GRID_W)
    return b.transpose(0, 1, 3, 2, 4, 5).reshape(DEPTH, NA_HEADS // 2, WIN_ROWS, 2 * GRID_W, WIN_ROWS * GRID_W)


def _na(p, w, l, *, seq):
    n = p["nq"].shape[0]
    nb = n // seq
    col = pl.BlockSpec((seq, LANES), lambda hp, b: (b, hp))
    return pl.pallas_call(
        functools.partial(_na_kernel, seq=seq),
        grid=(NA_HEADS // 2, nb),
        in_specs=[col, col, col,
                  pl.BlockSpec((None, 1, WIN_ROWS, 2 * GRID_W, WIN_ROWS * GRID_W),
                               lambda hp, b: (l, hp, 0, 0, 0))],
        out_specs=col,
        out_shape=jax.ShapeDtypeStruct((n, NA_WIDTH), BF16),
        compiler_params=_params("arbitrary", "arbitrary"),
        name="natten",
    )(p["nq"], p["nk"], p["nv"], w["bias"])


def _lower_bounds(p):
    sm = jax.nn.softmax(p.astype(F32), axis=0)
    return jnp.cumsum(sm, axis=0) - sm[0]


def _trunk(x, b0, w, consts):
    nb, seq, _ = x.shape
    x = x.reshape(nb * seq, D_MODEL)
    for l in range(DEPTH):
        x = _ffn(x, w, l, 0, seq=seq, b0=b0)
        p = _in_proj(x, w, l, seq=seq, b0=b0)
        of, ob = _hgrn(p, consts, seq=seq)
        ona = _na(p, w, l, seq=seq)
        x = _ffn(x, w, l, 1, seq=seq, b0=b0, mix=(of, ob, p["sg"], ona))
    return x.reshape(nb, seq, D_MODEL)


def kernel(x_prompt, x_sample, c_prompt, c_sample, w_mod, b_mod, norm_g, ffn_w_gate, ffn_w_up, ffn_w_down,
           w_in, w_out, hg_lb_fwd, hg_lb_bwd, hg_norm_g, na_q_norm_g, na_k_norm_g, na_rpb):
    nbp = x_prompt.shape[0]
    c_all = jnp.concatenate([c_prompt, c_sample], axis=0)
    per_head = lambda g, scale: jnp.tile(g.astype(F32) * scale, (1, NA_HEADS)).reshape(DEPTH, 1, NA_WIDTH)
    w = {
        "mod": _modulation(c_all, w_mod, b_mod).reshape(DEPTH, c_all.shape[0], 9, D_MODEL),
        "norm_g": norm_g.reshape(DEPTH, 3, 1, D_MODEL),
        "wg": ffn_w_gate.astype(BF16), "wu": ffn_w_up.astype(BF16), "wd": ffn_w_down.astype(BF16),
        "w_in": w_in.astype(BF16), "w_out": w_out.astype(BF16),
        "lb_f": _lower_bounds(hg_lb_fwd).reshape(DEPTH, 1, HG_WIDTH),
        "lb_b": _lower_bounds(hg_lb_bwd).reshape(DEPTH, 1, HG_WIDTH),
        "hg_norm_g": hg_norm_g.reshape(DEPTH, 1, LANES),
        "qg": per_head(na_q_norm_g, NA_DH ** -0.5 * LOG2E), "kg": per_head(na_k_norm_g, 1.0),
        "bias": _na_bias_tables(na_rpb),
    }
    consts = _hgrn_constants()
    y_prompt = _trunk(x_prompt, 0, w, consts)
    y_sample = _trunk(x_sample, nbp, w, consts)
    return (y_prompt, y_sample)
```

```python
import functools
import math

import numpy as np
import jax
import jax.numpy as jnp
from jax import lax
from jax.experimental import pallas as pl
from jax.experimental.pallas import tpu as pltpu

F32 = jnp.float32
BF16 = jnp.bfloat16

D_MODEL = 1024
DEPTH = 4
GRID_W = 64
HG_HEADS = 4
HG_DK = 128
HG_WIDTH = 512
NA_HEADS = 8
NA_DH = 64
NA_WIDTH = 512
WIN_ROWS = 8
WIN_COLS = 16
D_FF = 2816
IN_WIDTH = 4096
EPS = 1e-6
LOG2E = math.log2(math.e)

LANES = 128
SUBLANES = 8
TM = 512
HG_C = 128
HG_R = 1024
HG_AHEAD = 2
HG_LEVELS = tuple(HG_C >> i for i in range(HG_C.bit_length() - 1))
HG_COARSE = tuple(s for s in HG_LEVELS if s // 2 >= SUBLANES)
HG_FINE = tuple(s for s in HG_LEVELS if s // 2 < SUBLANES)
HG_GROUPS = 2 * (HG_HEADS // 2)
HG_CHAINS = 2 * HG_HEADS
NA_ROWS_PER_ITER = 8
NEG_BIG = -1e30
VMEM_LIMIT = 56 * 1024 * 1024

NT_DIMS = (((1,), (1,)), ((), ()))
TN_DIMS = (((0,), (0,)), ((), ()))


def _dot(a, b):
    return jnp.dot(a, b, preferred_element_type=F32)


def _dot_nt(a, b):
    return lax.dot_general(a, b, NT_DIMS, preferred_element_type=F32)


def _sigmoid(x):
    return 1.0 / (1.0 + jnp.exp(-x))


def _silu(x):
    return x * _sigmoid(x)


def _layer_spec(shape, *lead):
    tail = (0,) * len(shape)
    return pl.BlockSpec((None,) * len(lead) + tuple(shape), lambda *_: tuple(lead) + tail,
                        pipeline_mode=pl.Buffered(1))


def _const_spec(shape):
    return _layer_spec(shape)


def _params(*sem):
    return pltpu.CompilerParams(dimension_semantics=sem, vmem_limit_bytes=VMEM_LIMIT)


def _mod_kernel(c_ref, w_ref, b_ref, o_ref):
    a = _silu(c_ref[...]).astype(BF16)
    o_ref[0] = _dot(a, w_ref[0].astype(BF16)) + b_ref[0]


def _modulation(c_all, w_mod, b_mod):
    nb = c_all.shape[0]
    tn = 1152
    return pl.pallas_call(
        _mod_kernel,
        grid=(DEPTH, 9 * D_MODEL // tn),
        in_specs=[
            pl.BlockSpec((nb, D_MODEL), lambda l, j: (0, 0)),
            pl.BlockSpec((1, D_MODEL, tn), lambda l, j: (l, 0, j)),
            pl.BlockSpec((1, 1, tn), lambda l, j: (l, 0, j)),
        ],
        out_specs=pl.BlockSpec((1, nb, tn), lambda l, j: (l, 0, j)),
        out_shape=jax.ShapeDtypeStruct((DEPTH, nb, 9 * D_MODEL), F32),
        compiler_params=_params("arbitrary", "arbitrary"),
        name="modulation",
    )(c_all, w_mod, b_mod.reshape(DEPTH, 1, 9 * D_MODEL))


def _norm_modulate(x, g, shift, scale):
    ms = jnp.mean(x * x, axis=-1, keepdims=True)
    h = (x * lax.rsqrt(ms + EPS)) * g
    return h * (1.0 + scale) + shift


def _mod_spec(l, b0, seq):
    return pl.BlockSpec((None, 1, 9, D_MODEL), lambda i: (l, b0 + i * TM // seq, 0, 0))


def _ffn_kernel(*refs, sub, with_mix):
    if with_mix:
        (x_ref, of_ref, ob_ref, sg_ref, ona_ref, hng_ref, wout_ref,
         mod_ref, ng_ref, wg_ref, wu_ref, wd_ref, o_ref) = refs
    else:
        x_ref, mod_ref, ng_ref, wg_ref, wu_ref, wd_ref, o_ref = refs
    x = x_ref[...]
    if with_mix:
        heads = []
        for h in range(HG_HEADS):
            cols = slice(h * LANES, (h + 1) * LANES)
            o = of_ref[:, cols] + ob_ref[:, cols]
            ms = jnp.mean(o * o, axis=-1, keepdims=True)
            o = (o * lax.rsqrt(ms + EPS)) * hng_ref[...]
            heads.append((o * sg_ref[:, cols].astype(F32)).astype(BF16))
        ohg = jnp.concatenate(heads, axis=-1)
        mix = _dot(ohg, wout_ref[0:HG_WIDTH, :]) + _dot(ona_ref[...], wout_ref[HG_WIDTH:, :])
        x = x + mod_ref[0, 5:6, :] * mix
    shift = mod_ref[0, 3 * sub:3 * sub + 1, :]
    scale = mod_ref[0, 3 * sub + 1:3 * sub + 2, :]
    gate = mod_ref[0, 3 * sub + 2:3 * sub + 3, :]
    h = _norm_modulate(x, ng_ref[...], shift, scale).astype(BF16)
    g = _dot(h, wg_ref[...])
    u = _dot(h, wu_ref[...])
    a = (_silu(g) * u).astype(BF16)
    y = _dot(a, wd_ref[...])
    o_ref[...] = x + 0.5 * gate * y


def _ffn(x, w, l, ffn_idx, *, seq, b0, mix=None):
    n = x.shape[0]
    sub = 2 * ffn_idx
    tok = lambda width: pl.BlockSpec((TM, width), lambda i: (i, 0))
    args, specs = [x], [tok(D_MODEL)]
    if mix is not None:
        of, ob, sg, ona = mix
        args += [of, ob, sg, ona, w["hg_norm_g"], w["w_out"]]
        specs += [tok(HG_WIDTH), tok(HG_WIDTH), tok(HG_WIDTH), tok(NA_WIDTH),
                  _layer_spec((1, LANES), l), _layer_spec((D_MODEL, D_MODEL), l)]
    args += [w["mod"], w["norm_g"], w["wg"], w["wu"], w["wd"]]
    specs += [_mod_spec(l, b0, seq), _layer_spec((1, D_MODEL), l, sub),
              _layer_spec((D_MODEL, D_FF), l, ffn_idx), _layer_spec((D_MODEL, D_FF), l, ffn_idx),
              _layer_spec((D_FF, D_MODEL), l, ffn_idx)]
    return pl.pallas_call(
        functools.partial(_ffn_kernel, sub=sub, with_mix=mix is not None),
        grid=(n // TM,),
        in_specs=specs,
        out_specs=tok(D_MODEL),
        out_shape=jax.ShapeDtypeStruct((n, D_MODEL), F32),
        compiler_params=_params("arbitrary"),
        name="ffn_mix" if mix is not None else "ffn",
    )(*args)


PROJ_OUTS = ("q", "kf", "kb", "gfh", "gfl", "gbh", "gbl", "v", "sg", "nq", "nk", "nv")


def _proj_kernel(x_ref, mod_ref, ng_ref, w_ref, lbf_ref, lbb_ref, qg_ref, kg_ref, *outs):
    o = dict(zip(PROJ_OUTS, outs))
    h = _norm_modulate(x_ref[...], ng_ref[...], mod_ref[0, 3:4, :], mod_ref[0, 4:5, :]).astype(BF16)
    seg = lambda i: _dot(h, w_ref[:, i * HG_WIDTH:(i + 1) * HG_WIDTH])

    def decay(i, lb_ref, k_ref, hi_ref, lo_ref):
        lb = lb_ref[...]
        f = lb + (1.0 - lb) * _sigmoid(seg(i))
        k_ref[...] = (1.0 - f).astype(BF16)
        g = jnp.log2(f)
        hi = g.astype(BF16)
        hi_ref[...] = hi
        lo_ref[...] = (g - hi.astype(F32)).astype(BF16)

    lane = lax.broadcasted_iota(jnp.int32, (TM, LANES), 1)
    first = lane < NA_DH

    def head_norm(i, g_ref, o_ref):
        x = seg(i)
        for t in range(NA_WIDTH // LANES):
            cols = slice(t * LANES, (t + 1) * LANES)
            xt = x[:, cols]
            x2 = xt * xt
            sa = jnp.sum(jnp.where(first, x2, 0.0), axis=-1, keepdims=True)
            sb = jnp.sum(jnp.where(first, 0.0, x2), axis=-1, keepdims=True)
            ms = jnp.where(first, sa, sb) * (1.0 / NA_DH)
            o_ref[:, cols] = ((xt * lax.rsqrt(ms + EPS)) * g_ref[:, cols]).astype(BF16)

    decay(1, lbf_ref, o["kf"], o["gfh"], o["gfl"])
    decay(2, lbb_ref, o["kb"], o["gbh"], o["gbl"])
    head_norm(5, qg_ref, o["nq"])
    head_norm(6, kg_ref, o["nk"])
    o["q"][...] = _silu(seg(0)).astype(BF16)
    o["sg"][...] = _silu(seg(4)).astype(BF16)
    o["v"][...] = seg(3).astype(BF16)
    o["nv"][...] = seg(7).astype(BF16)


def _in_proj(x, w, l, *, seq, b0):
    n = x.shape[0]
    vec = _layer_spec((1, HG_WIDTH), l)
    outs = pl.pallas_call(
        _proj_kernel,
        grid=(n // TM,),
        in_specs=[
            pl.BlockSpec((TM, D_MODEL), lambda i: (i, 0)),
            _mod_spec(l, b0, seq),
            _layer_spec((1, D_MODEL), l, 1),
            _layer_spec((D_MODEL, IN_WIDTH), l),
            vec, vec, vec, vec,
        ],
        out_specs=[pl.BlockSpec((TM, HG_WIDTH), lambda i: (i, 0))] * len(PROJ_OUTS),
        out_shape=[jax.ShapeDtypeStruct((n, HG_WIDTH), BF16)] * len(PROJ_OUTS),
        compiler_params=_params("arbitrary"),
        name="in_proj",
    )(x, w["mod"], w["norm_g"], w["w_in"], w["lb_f"], w["lb_b"], w["qg"], w["kg"])
    return dict(zip(PROJ_OUTS, outs))


def _hgrn_constants():
    c = HG_C
    idx = np.arange(c)
    t = idx[None, :]
    i = idx[:, None]
    run_blocks = [t <= i]
    sels, masks = [], []
    for s in HG_LEVELS:
        half = s // 2
        p = idx % s
        start = idx - p
        m = (start + half - 1)[:, None]
        is_q = p >= half
        masks.append(is_q[:, None] & ~is_q[None, :] & (start[:, None] == start[None, :]))
        if s in HG_FINE:
            qrow = (t >= m + 1) & (t <= i)
            krow = (t >= i + 1) & (t <= m)
            run_blocks.append(np.where(is_q[:, None], qrow, krow))
            sels.append(np.broadcast_to(is_q[:, None], (c, LANES)))
    masks.append(np.eye(c, dtype=bool))
    assert len(masks) % 2 == 0
    both = lambda a: np.stack([a, a[:, ::-1, ::-1]])
    run = both(np.stack(run_blocks).astype(np.float32))
    run = np.concatenate([run, run], axis=-1).reshape(2, -1, 2 * c)
    msk = both(np.stack(masks).astype(np.float32))
    msk = msk.reshape(2, len(masks) // 2, 2, c, c).transpose(0, 1, 3, 2, 4).reshape(2, -1, c, 2 * c)
    sel = np.stack(sels).astype(np.float32)
    sel = np.stack([sel, sel[:, ::-1, :]])
    return jnp.asarray(run, dtype=BF16), jnp.asarray(msk, dtype=BF16), jnp.asarray(sel, dtype=F32)


def _hgrn_operands(d, b, a_fine, q, kk, kkb, selq, x_ref, xt_ref, e_ref):
    c = HG_C
    nlev = len(HG_LEVELS)

    def put(l, x):
        xb = x.astype(BF16)
        x_ref[l] = xb
        xt_ref[l] = xb.T

    for l, s in enumerate(HG_COARSE):
        half = s // 2
        parts = []
        for blk in range(c // half):
            rows = slice(blk * half, (blk + 1) * half)
            base = (blk // 2) * s
            if d == 0:
                m, is_q = base + half - 1, blk % 2 == 1
            else:
                m, is_q = base + half, blk % 2 == 0
            bm = b[m:m + 1]
            if is_q:
                parts.append(q[rows] * jnp.exp2(b[rows] - bm))
            else:
                parts.append(kk[rows] * jnp.exp2(bm - b[rows]))
        put(l, jnp.concatenate(parts, axis=0))
    for lf in range(len(HG_FINE)):
        e = jnp.exp2(a_fine[lf * c:(lf + 1) * c])
        put(len(HG_COARSE) + lf, jnp.where(selq[lf], q, kk) * e)
    last = c - 1 if d == 0 else 0
    b_tot = b[last:last + 1]
    x_ref[nlev] = (q * jnp.exp2(b)).astype(BF16)
    xt_ref[nlev] = kkb.T
    xt_ref[nlev + 1] = (kk * jnp.exp2(b_tot - b)).astype(BF16).T
    e_ref[...] = jnp.broadcast_to(jnp.exp2(b_tot), e_ref.shape)


def _hgrn_kernel(qf_ref, kf_ref, gfh_ref, gfl_ref, vf_ref, qb_ref, kb_ref, gbh_ref, gbl_ref, vb_ref,
                 nqf_ref, nkf_ref, ngfh_ref, ngfl_ref, nqb_ref, nkb_ref, ngbh_ref, ngbl_ref,
                 run_ref, msk_ref, sel_ref, of_ref, ob_ref, st_ref,
                 a0_ref, a1_ref, x0_ref, x1_ref, xt0_ref, xt1_ref, e0_ref, e1_ref):
    c = HG_C
    nchunk = HG_R // c
    assert nchunk % 2 == 0 and nchunk >= 2 * HG_AHEAD
    nlev = len(HG_LEVELS)
    npair = (nlev + 1) // 2
    a_refs, x_refs, e_refs = (a0_ref, a1_ref), (x0_ref, x1_ref), (e0_ref, e1_ref)
    xt_refs = (xt0_ref, xt1_ref)

    cur = ((qf_ref, kf_ref, gfh_ref, gfl_ref), (qb_ref, kb_ref, gbh_ref, gbl_ref))
    nxt = ((nqf_ref, nkf_ref, ngfh_ref, ngfl_ref), (nqb_ref, nkb_ref, ngbh_ref, ngbl_ref))
    vo = ((vf_ref, of_ref), (vb_ref, ob_ref))
    zero = jnp.zeros((c, LANES), BF16)

    def chunk_rows(j, d):
        jj = j if d == 0 else nchunk - 1 - j
        return pl.ds(pl.multiple_of(jj * c, c), c)

    def cur_src(j):
        return lambda d: (cur[d], chunk_rows(j, d))

    def nxt_src(jn):
        return lambda d: (nxt[d], pl.ds((jn if d == 0 else HG_AHEAD - 1 - jn) * c, c))

    def run_sums(src, slot):
        for d in range(2):
            (_, _, gh_ref, gl_ref), rows = src(d)
            for hp in range(HG_HEADS // 2):
                cols2 = slice(hp * 2 * LANES, (hp + 1) * 2 * LANES)
                g2 = jnp.concatenate([gh_ref[rows, cols2], gl_ref[rows, cols2]], axis=0)
                a_refs[slot][2 * d + hp] = _dot(run_ref[d], g2)

    def operands(src, slot, d, h, selq):
        (q_ref, k_ref, _, _), rows = src(d)
        cols = slice(h * LANES, (h + 1) * LANES)
        lanes = slice((h % 2) * LANES, (h % 2 + 1) * LANES)
        a_ref = a_refs[slot]
        ci = d * HG_HEADS + h
        kkb = k_ref[rows, cols]
        _hgrn_operands(d, a_ref[2 * d + h // 2, 0:c, lanes], a_ref[2 * d + h // 2, c:, lanes],
                       q_ref[rows, cols].astype(F32), kkb.astype(F32), kkb, selq,
                       x_refs[slot].at[ci], xt_refs[slot].at[ci], e_refs[slot].at[ci])

    def level_pair(j, slot, d, h, i):
        q_ref = cur[d][0]
        rows = chunk_rows(j, d)
        cols = slice(h * LANES, (h + 1) * LANES)
        ci = d * HG_HEADS + h
        x_ref, xt_ref = x_refs[slot].at[ci], xt_refs[slot].at[ci]
        l2 = x_ref[2 * i + 1] if 2 * i + 1 < nlev else q_ref[rows, cols]
        g2 = _dot(jnp.concatenate([x_ref[2 * i], l2], axis=1),
                  jnp.concatenate([jnp.concatenate([xt_ref[2 * i], zero], axis=1),
                                   jnp.concatenate([zero, xt_ref[2 * i + 1]], axis=1)], axis=0))
        g2 = g2.astype(BF16) * msk_ref[d, i]
        return g2[:, :c] + g2[:, c:]

    def finish(j, slot, d, h, p):
        v_ref, o_ref = vo[d]
        rows = chunk_rows(j, d)
        cols = slice(h * LANES, (h + 1) * LANES)
        ci = d * HG_HEADS + h
        vb = v_ref[rows, cols]
        st = st_ref[d, h]
        o_ref[rows, cols] = _dot(jnp.concatenate([p, x_refs[slot][ci, nlev]], axis=1),
                                 jnp.concatenate([vb, st.astype(BF16)], axis=0))
        e_col = jnp.transpose(jnp.broadcast_to(e_refs[slot][ci, 0:1, :], (c, LANES)))
        st_ref[d, h] = st * e_col + _dot(xt_refs[slot][ci, nlev + 1], vb)

    def sel_masks():
        return [[sel_ref[d, lf] != 0.0 for lf in range(len(HG_FINE))] for d in range(2)]

    chains = [(d, h) for d in range(2) for h in range(HG_HEADS)]
    assert len(chains) % npair == 0

    def step(j, slot, ahead2, ahead1):
        run_sums(ahead2, slot)
        selq = sel_masks()
        ps = [None] * len(chains)
        per = len(chains) // npair
        for i in range(npair):
            for ci, (d, h) in enumerate(chains):
                g = level_pair(j, slot, d, h, i)
                ps[ci] = g if ps[ci] is None else ps[ci] + g
            for d, h in chains[i * per:(i + 1) * per]:
                operands(ahead1, 1 - slot, d, h, selq[d])
        for ci, (d, h) in enumerate(chains):
            finish(j, slot, d, h, ps[ci])

    def body(i, carry):
        step(2 * i, 0, cur_src(2 * i + 2), cur_src(2 * i + 1))
        step(2 * i + 1, 1, cur_src(2 * i + 3), cur_src(2 * i + 2))
        return carry

    @pl.when(pl.program_id(1) == 0)
    def _():
        st_ref[...] = jnp.zeros_like(st_ref)
        run_sums(cur_src(0), 0)
        run_sums(cur_src(1), 1)
        selq = sel_masks()
        for d, h in chains:
            operands(cur_src(0), 0, d, h, selq[d])

    lax.fori_loop(0, nchunk // 2 - 1, body, 0)
    step(nchunk - 2, 0, nxt_src(0), cur_src(nchunk - 1))
    step(nchunk - 1, 1, nxt_src(1), nxt_src(0))


def _hgrn(p, consts, *, seq):
    n = p["q"].shape[0]
    nb, nt = n // seq, seq // HG_R
    run, msk, sel = consts
    ahead = HG_AHEAD * HG_C
    per = HG_R // ahead
    fwd = pl.BlockSpec((HG_R, HG_WIDTH), lambda b, t: (b * nt + t, 0))
    bwd = pl.BlockSpec((HG_R, HG_WIDTH), lambda b, t: (b * nt + nt - 1 - t, 0))
    nfwd = pl.BlockSpec((ahead, HG_WIDTH), lambda b, t: ((b * nt + jnp.minimum(t + 1, nt - 1)) * per, 0))
    nbwd = pl.BlockSpec((ahead, HG_WIDTH),
                        lambda b, t: ((b * nt + jnp.maximum(nt - 2 - t, 0)) * per + per - 1, 0))
    out = jax.ShapeDtypeStruct((n, HG_WIDTH), F32)
    return pl.pallas_call(
        _hgrn_kernel,
        grid=(nb, nt),
        in_specs=[fwd] * 5 + [bwd] * 5 + [nfwd] * 4 + [nbwd] * 4
        + [_const_spec(run.shape), _const_spec(msk.shape), _const_spec(sel.shape)],
        out_specs=[fwd, bwd],
        out_shape=[out, out],
        scratch_shapes=[pltpu.VMEM((2, HG_HEADS, HG_DK, LANES), F32)]
        + [pltpu.VMEM((HG_GROUPS, run.shape[1], 2 * LANES), F32)] * 2
        + [pltpu.VMEM((HG_CHAINS, len(HG_LEVELS) + 1, HG_C, LANES), BF16)] * 2
        + [pltpu.VMEM((HG_CHAINS, len(HG_LEVELS) + 2, HG_DK, HG_C), BF16)] * 2
        + [pltpu.VMEM((HG_CHAINS, SUBLANES, LANES), F32)] * 2,
        compiler_params=_params("arbitrary", "arbitrary"),
        name="hgrn2",
    )(p["q"], p["kf"], p["gfh"], p["gfl"], p["v"], p["q"], p["kb"], p["gbh"], p["gbl"], p["v"],
      p["q"], p["kf"], p["gfh"], p["gfl"], p["q"], p["kb"], p["gbh"], p["gbl"], run, msk, sel)


def _na_kernel(q_ref, k_ref, v_ref, bias_ref, o_ref, *, seq):
    rows = seq // GRID_W
    win = WIN_ROWS * GRID_W
    lane = lax.broadcasted_iota(jnp.int32, (GRID_W, LANES), 1)
    first = lane < NA_DH

    def body(i, carry):
        rs = [i * NA_ROWS_PER_ITER + u for u in range(NA_ROWS_PER_ITER)]
        r0s = [jnp.clip(r - WIN_ROWS // 2, 0, rows - WIN_ROWS) for r in rs]
        qrows = [pl.ds(pl.multiple_of(r * GRID_W, GRID_W), GRID_W) for r in rs]
        wrows = [pl.ds(pl.multiple_of(r0 * GRID_W, GRID_W), win) for r0 in r0s]
        ss = []
        for r, r0, qr_, wr_ in zip(rs, r0s, qrows, wrows):
            qr = q_ref[qr_, :]
            zero = jnp.zeros_like(qr)
            q2 = jnp.concatenate([jnp.where(first, qr, zero), jnp.where(first, zero, qr)], axis=0)
            ss.append(_dot_nt(q2, k_ref[wr_, :]) + bias_ref[0, r - r0])
        ps, ls = [], []
        for s in ss:
            p = jnp.exp2(s - jnp.max(s, axis=-1, keepdims=True))
            ls.append(jnp.sum(p, axis=-1, keepdims=True))
            ps.append(p.astype(BF16))
        for p, l, qr_, wr_ in zip(ps, ls, qrows, wrows):
            o2 = _dot(p, v_ref[wr_, :]) * (1.0 / l)
            o_ref[qr_, :] = jnp.where(first, o2[:GRID_W], o2[GRID_W:]).astype(BF16)
        return carry

    lax.fori_loop(0, rows // NA_ROWS_PER_ITER, body, 0)


def _na_bias_tables(na_rpb):
    qcol = np.arange(GRID_W)
    c0 = np.clip(qcol - WIN_COLS // 2, 0, GRID_W - WIN_COLS)
    col_mask = (qcol[None, :] >= c0[:, None]) & (qcol[None, :] < c0[:, None] + WIN_COLS)
    rel_c = np.clip(qcol[None, :] - qcol[:, None], -(WIN_COLS - 1), WIN_COLS - 1) + WIN_COLS - 1
    onehot = (rel_c[:, :, None] == np.arange(2 * WIN_COLS - 1)).astype(np.float32)
    t = jnp.einsum("qkc,lhrc->lhrqk", onehot, na_rpb.astype(F32), precision=lax.Precision.HIGHEST)
    t = jnp.where(col_mask, t * LOG2E, NEG_BIG)
    per_off = [t[:, :, WIN_ROWS - 1 - dl:2 * WIN_ROWS - 1 - dl] for dl in range(WIN_ROWS)]
    b = jnp.stack(per_off, axis=2)
    b = b.transpose(0, 1, 2, 4, 3, 5).reshape(DEPTH, NA_HEADS // 2, 2, WIN_ROWS, GRID_W, WIN_ROWS * GRID_W)
    return b.transpose(0, 1, 3, 2, 4, 5).reshape(DEPTH, NA_HEADS // 2, WIN_ROWS, 2 * GRID_W, WIN_ROWS * GRID_W)


def _na(p, w, l, *, seq):
    n = p["nq"].shape[0]
    nb = n // seq
    col = pl.BlockSpec((seq, LANES), lambda hp, b: (b, hp))
    return pl.pallas_call(
        functools.partial(_na_kernel, seq=seq),
        grid=(NA_HEADS // 2, nb),
        in_specs=[col, col, col,
                  pl.BlockSpec((None, 1, WIN_ROWS, 2 * GRID_W, WIN_ROWS * GRID_W),
                               lambda hp, b: (l, hp, 0, 0, 0))],
        out_specs=col,
        out_shape=jax.ShapeDtypeStruct((n, NA_WIDTH), BF16),
        compiler_params=_params("arbitrary", "arbitrary"),
        name="natten",
    )(p["nq"], p["nk"], p["nv"], w["bias"])


def _lower_bounds(p):
    sm = jax.nn.softmax(p.astype(F32), axis=0)
    return jnp.cumsum(sm, axis=0) - sm[0]


def _trunk(x, b0, w, consts):
    nb, seq, _ = x.shape
    x = x.reshape(nb * seq, D_MODEL)
    for l in range(DEPTH):
        x = _ffn(x, w, l, 0, seq=seq, b0=b0)
        p = _in_proj(x, w, l, seq=seq, b0=b0)
        of, ob = _hgrn(p, consts, seq=seq)
        ona = _na(p, w, l, seq=seq)
        x = _ffn(x, w, l, 1, seq=seq, b0=b0, mix=(of, ob, p["sg"], ona))
    return x.reshape(nb, seq, D_MODEL)


def kernel(x_prompt, x_sample, c_prompt, c_sample, w_mod, b_mod, norm_g, ffn_w_gate, ffn_w_up, ffn_w_down,
           w_in, w_out, hg_lb_fwd, hg_lb_bwd, hg_norm_g, na_q_norm_g, na_k_norm_g, na_rpb):
    nbp = x_prompt.shape[0]
    c_all = jnp.concatenate([c_prompt, c_sample], axis=0)
    per_head = lambda g, scale: jnp.tile(g.astype(F32) * scale, (1, NA_HEADS)).reshape(DEPTH, 1, NA_WIDTH)
    w = {
        "mod": _modulation(c_all, w_mod, b_mod).reshape(DEPTH, c_all.shape[0], 9, D_MODEL),
        "norm_g": norm_g.reshape(DEPTH, 3, 1, D_MODEL),
        "wg": ffn_w_gate.astype(BF16), "wu": ffn_w_up.astype(BF16), "wd": ffn_w_down.astype(BF16),
        "w_in": w_in.astype(BF16), "w_out": w_out.astype(BF16),
        "lb_f": _lower_bounds(hg_lb_fwd).reshape(DEPTH, 1, HG_WIDTH),
        "lb_b": _lower_bounds(hg_lb_bwd).reshape(DEPTH, 1, HG_WIDTH),
        "hg_norm_g": hg_norm_g.reshape(DEPTH, 1, LANES),
        "qg": per_head(na_q_norm_g, NA_DH ** -0.5 * LOG2E), "kg": per_head(na_k_norm_g, 1.0),
        "bias": _na_bias_tables(na_rpb),
    }
    consts = _hgrn_constants()
    y_prompt = _trunk(x_prompt, 0, w, consts)
    y_sample = _trunk(x_sample, nbp, w, consts)
    return (y_prompt, y_sample)
```

```python
import functools
import math

import numpy as np
import jax
import jax.numpy as jnp
from jax import lax
from jax.experimental import pallas as pl
from jax.experimental.pallas import tpu as pltpu

F32 = jnp.float32
BF16 = jnp.bfloat16

D_MODEL = 1024
DEPTH = 4
GRID_W = 64
HG_HEADS = 4
HG_DK = 128
HG_WIDTH = 512
NA_HEADS = 8
NA_DH = 64
NA_WIDTH = 512
WIN_ROWS = 8
WIN_COLS = 16
D_FF = 2816
IN_WIDTH = 4096
EPS = 1e-6
LOG2E = math.log2(math.e)

LANES = 128
SUBLANES = 8
TM = 512
TM_PROJ = 1024
HG_C = 128
HG_R = 1024
HG_AHEAD = 2
HG_LEVELS = tuple(HG_C >> i for i in range(HG_C.bit_length() - 1))
HG_COARSE = tuple(s for s in HG_LEVELS if s // 2 >= SUBLANES)
HG_FINE = tuple(s for s in HG_LEVELS if s // 2 < SUBLANES)
HG_GROUPS = 2 * (HG_HEADS // 2)
HG_CHAINS = 2 * HG_HEADS
NA_ROWS_PER_ITER = 8
NEG_BIG = -1e30
VMEM_LIMIT = 56 * 1024 * 1024

NT_DIMS = (((1,), (1,)), ((), ()))
TN_DIMS = (((0,), (0,)), ((), ()))


def _dot(a, b):
    return jnp.dot(a, b, preferred_element_type=F32)


def _dot_nt(a, b):
    return lax.dot_general(a, b, NT_DIMS, preferred_element_type=F32)


def _sigmoid(x):
    return 1.0 / (1.0 + jnp.exp(-x))


def _silu(x):
    return x * _sigmoid(x)


def _layer_spec(shape, *lead):
    tail = (0,) * len(shape)
    return pl.BlockSpec((None,) * len(lead) + tuple(shape), lambda *_: tuple(lead) + tail,
                        pipeline_mode=pl.Buffered(1))


def _const_spec(shape):
    return _layer_spec(shape)


def _params(*sem):
    return pltpu.CompilerParams(dimension_semantics=sem, vmem_limit_bytes=VMEM_LIMIT)


def _mod_kernel(c_ref, w_ref, b_ref, o_ref):
    a = _silu(c_ref[...]).astype(BF16)
    o_ref[0] = _dot(a, w_ref[0].astype(BF16)) + b_ref[0]


def _modulation(c_all, w_mod, b_mod):
    nb = c_all.shape[0]
    tn = 1152
    return pl.pallas_call(
        _mod_kernel,
        grid=(DEPTH, 9 * D_MODEL // tn),
        in_specs=[
            pl.BlockSpec((nb, D_MODEL), lambda l, j: (0, 0)),
            pl.BlockSpec((1, D_MODEL, tn), lambda l, j: (l, 0, j)),
            pl.BlockSpec((1, 1, tn), lambda l, j: (l, 0, j)),
        ],
        out_specs=pl.BlockSpec((1, nb, tn), lambda l, j: (l, 0, j)),
        out_shape=jax.ShapeDtypeStruct((DEPTH, nb, 9 * D_MODEL), F32),
        compiler_params=_params("arbitrary", "arbitrary"),
        name="modulation",
    )(c_all, w_mod, b_mod.reshape(DEPTH, 1, 9 * D_MODEL))


def _norm_modulate(x, g, shift, scale):
    ms = jnp.mean(x * x, axis=-1, keepdims=True)
    h = (x * lax.rsqrt(ms + EPS)) * g
    return h * (1.0 + scale) + shift


def _mod_spec(l, b0, seq, tm):
    return pl.BlockSpec((None, 1, 9, D_MODEL), lambda i: (l, b0 + i * tm // seq, 0, 0))


def _ffn_kernel(*refs, sub, with_mix):
    if with_mix:
        (x_ref, of_ref, ob_ref, sg_ref, ona_ref, hng_ref, wout_ref,
         mod_ref, ng_ref, wg_ref, wu_ref, wd_ref, o_ref) = refs
    else:
        x_ref, mod_ref, ng_ref, wg_ref, wu_ref, wd_ref, o_ref = refs
    x = x_ref[...]
    if with_mix:
        heads = []
        for h in range(HG_HEADS):
            cols = slice(h * LANES, (h + 1) * LANES)
            o = of_ref[:, cols] + ob_ref[:, cols]
            ms = jnp.mean(o * o, axis=-1, keepdims=True)
            o = (o * lax.rsqrt(ms + EPS)) * hng_ref[...]
            heads.append((o * sg_ref[:, cols].astype(F32)).astype(BF16))
        ohg = jnp.concatenate(heads, axis=-1)
        mix = _dot(ohg, wout_ref[0:HG_WIDTH, :]) + _dot(ona_ref[...], wout_ref[HG_WIDTH:, :])
        x = x + mod_ref[0, 5:6, :] * mix
    shift = mod_ref[0, 3 * sub:3 * sub + 1, :]
    scale = mod_ref[0, 3 * sub + 1:3 * sub + 2, :]
    gate = mod_ref[0, 3 * sub + 2:3 * sub + 3, :]
    h = _norm_modulate(x, ng_ref[...], shift, scale).astype(BF16)
    g = _dot(h, wg_ref[...])
    u = _dot(h, wu_ref[...])
    a = (_silu(g) * u).astype(BF16)
    y = _dot(a, wd_ref[...])
    o_ref[...] = x + 0.5 * gate * y


def _ffn(x, w, l, ffn_idx, *, seq, b0, mix=None):
    n = x.shape[0]
    sub = 2 * ffn_idx
    tok = lambda width: pl.BlockSpec((TM, width), lambda i: (i, 0))
    args, specs = [x], [tok(D_MODEL)]
    if mix is not None:
        of, ob, sg, ona = mix
        args += [of, ob, sg, ona, w["hg_norm_g"], w["w_out"]]
        specs += [tok(HG_WIDTH), tok(HG_WIDTH), tok(HG_WIDTH), tok(NA_WIDTH),
                  _layer_spec((1, LANES), l), _layer_spec((D_MODEL, D_MODEL), l)]
    args += [w["mod"], w["norm_g"], w["wg"], w["wu"], w["wd"]]
    specs += [_mod_spec(l, b0, seq, TM), _layer_spec((1, D_MODEL), l, sub),
              _layer_spec((D_MODEL, D_FF), l, ffn_idx), _layer_spec((D_MODEL, D_FF), l, ffn_idx),
              _layer_spec((D_FF, D_MODEL), l, ffn_idx)]
    return pl.pallas_call(
        functools.partial(_ffn_kernel, sub=sub, with_mix=mix is not None),
        grid=(n // TM,),
        in_specs=specs,
        out_specs=tok(D_MODEL),
        out_shape=jax.ShapeDtypeStruct((n, D_MODEL), F32),
        compiler_params=_params("arbitrary"),
        name="ffn_mix" if mix is not None else "ffn",
    )(*args)


PROJ_OUTS = ("q", "kf", "kb", "gfh", "gfl", "gbh", "gbl", "v", "sg", "nq", "nk", "nv")


def _proj_kernel(x_ref, mod_ref, ng_ref, w_ref, lbf_ref, lbb_ref, qg_ref, kg_ref, *outs):
    o = dict(zip(PROJ_OUTS, outs))
    h = _norm_modulate(x_ref[...], ng_ref[...], mod_ref[0, 3:4, :], mod_ref[0, 4:5, :]).astype(BF16)
    seg = lambda i: _dot(h, w_ref[:, i * HG_WIDTH:(i + 1) * HG_WIDTH])

    def decay(i, lb_ref, k_ref, hi_ref, lo_ref):
        lb = lb_ref[...]
        f = lb + (1.0 - lb) * _sigmoid(seg(i))
        k_ref[...] = (1.0 - f).astype(BF16)
        g = jnp.log2(f)
        hi = g.astype(BF16)
        hi_ref[...] = hi
        lo_ref[...] = (g - hi.astype(F32)).astype(BF16)

    lane = lax.broadcasted_iota(jnp.int32, (x_ref.shape[0], LANES), 1)
    first = lane < NA_DH

    def head_norm(i, g_ref, o_ref):
        x = seg(i)
        for t in range(NA_WIDTH // LANES):
            cols = slice(t * LANES, (t + 1) * LANES)
            xt = x[:, cols]
            x2 = xt * xt
            sa = jnp.sum(jnp.where(first, x2, 0.0), axis=-1, keepdims=True)
            sb = jnp.sum(jnp.where(first, 0.0, x2), axis=-1, keepdims=True)
            ms = jnp.where(first, sa, sb) * (1.0 / NA_DH)
            o_ref[:, cols] = ((xt * lax.rsqrt(ms + EPS)) * g_ref[:, cols]).astype(BF16)

    decay(1, lbf_ref, o["kf"], o["gfh"], o["gfl"])
    decay(2, lbb_ref, o["kb"], o["gbh"], o["gbl"])
    head_norm(5, qg_ref, o["nq"])
    head_norm(6, kg_ref, o["nk"])
    o["q"][...] = _silu(seg(0)).astype(BF16)
    o["sg"][...] = _silu(seg(4)).astype(BF16)
    o["v"][...] = seg(3).astype(BF16)
    o["nv"][...] = seg(7).astype(BF16)


def _in_proj(x, w, l, *, seq, b0):
    n = x.shape[0]
    vec = _layer_spec((1, HG_WIDTH), l)
    outs = pl.pallas_call(
        _proj_kernel,
        grid=(n // TM_PROJ,),
        in_specs=[
            pl.BlockSpec((TM_PROJ, D_MODEL), lambda i: (i, 0)),
            _mod_spec(l, b0, seq, TM_PROJ),
            _layer_spec((1, D_MODEL), l, 1),
            _layer_spec((D_MODEL, IN_WIDTH), l),
            vec, vec, vec, vec,
        ],
        out_specs=[pl.BlockSpec((TM_PROJ, HG_WIDTH), lambda i: (i, 0))] * len(PROJ_OUTS),
        out_shape=[jax.ShapeDtypeStruct((n, HG_WIDTH), BF16)] * len(PROJ_OUTS),
        compiler_params=_params("arbitrary"),
        name="in_proj",
    )(x, w["mod"], w["norm_g"], w["w_in"], w["lb_f"], w["lb_b"], w["qg"], w["kg"])
    return dict(zip(PROJ_OUTS, outs))


def _hgrn_constants():
    c = HG_C
    idx = np.arange(c)
    t = idx[None, :]
    i = idx[:, None]
    run_blocks = [t <= i]
    sels, masks = [], []
    for s in HG_LEVELS:
        half = s // 2
        p = idx % s
        start = idx - p
        m = (start + half - 1)[:, None]
        is_q = p >= half
        masks.append(is_q[:, None] & ~is_q[None, :] & (start[:, None] == start[None, :]))
        if s in HG_FINE:
            qrow = (t >= m + 1) & (t <= i)
            krow = (t >= i + 1) & (t <= m)
            run_blocks.append(np.where(is_q[:, None], qrow, krow))
            sels.append(np.broadcast_to(is_q[:, None], (c, LANES)))
    masks.append(np.eye(c, dtype=bool))
    assert len(masks) % 2 == 0
    both = lambda a: np.stack([a, a[:, ::-1, ::-1]])
    run = both(np.stack(run_blocks).astype(np.float32))
    run = np.concatenate([run, run], axis=-1).reshape(2, -1, 2 * c)
    msk = both(np.stack(masks).astype(np.float32))
    msk = msk.reshape(2, len(masks) // 2, 2, c, c).transpose(0, 1, 3, 2, 4).reshape(2, -1, c, 2 * c)
    sel = np.stack(sels).astype(np.float32)
    sel = np.stack([sel, sel[:, ::-1, :]])
    return jnp.asarray(run, dtype=BF16), jnp.asarray(msk, dtype=BF16), jnp.asarray(sel, dtype=F32)


def _hgrn_operands(d, b, a_fine, q, kk, kkb, selq, x_ref, xt_ref, e_ref):
    c = HG_C
    nlev = len(HG_LEVELS)

    def put(l, x):
        xb = x.astype(BF16)
        x_ref[l] = xb
        xt_ref[l] = xb.T

    for l, s in enumerate(HG_COARSE):
        half = s // 2
        parts = []
        for blk in range(c // half):
            rows = slice(blk * half, (blk + 1) * half)
            base = (blk // 2) * s
            if d == 0:
                m, is_q = base + half - 1, blk % 2 == 1
            else:
                m, is_q = base + half, blk % 2 == 0
            bm = b[m:m + 1]
            if is_q:
                parts.append(q[rows] * jnp.exp2(b[rows] - bm))
            else:
                parts.append(kk[rows] * jnp.exp2(bm - b[rows]))
        put(l, jnp.concatenate(parts, axis=0))
    for lf in range(len(HG_FINE)):
        e = jnp.exp2(a_fine[lf * c:(lf + 1) * c])
        put(len(HG_COARSE) + lf, jnp.where(selq[lf], q, kk) * e)
    last = c - 1 if d == 0 else 0
    b_tot = b[last:last + 1]
    x_ref[nlev] = (q * jnp.exp2(b)).astype(BF16)
    xt_ref[nlev] = kkb.T
    xt_ref[nlev + 1] = (kk * jnp.exp2(b_tot - b)).astype(BF16).T
    e_ref[...] = jnp.broadcast_to(jnp.exp2(b_tot), e_ref.shape)


def _hgrn_kernel(qf_ref, kf_ref, gfh_ref, gfl_ref, vf_ref, qb_ref, kb_ref, gbh_ref, gbl_ref, vb_ref,
                 nqf_ref, nkf_ref, ngfh_ref, ngfl_ref, nqb_ref, nkb_ref, ngbh_ref, ngbl_ref,
                 run_ref, msk_ref, sel_ref, of_ref, ob_ref, st_ref,
                 a0_ref, a1_ref, x0_ref, x1_ref, xt0_ref, xt1_ref, e0_ref, e1_ref):
    c = HG_C
    nchunk = HG_R // c
    assert nchunk % 2 == 0 and nchunk >= 2 * HG_AHEAD
    nlev = len(HG_LEVELS)
    npair = (nlev + 1) // 2
    a_refs, x_refs, e_refs = (a0_ref, a1_ref), (x0_ref, x1_ref), (e0_ref, e1_ref)
    xt_refs = (xt0_ref, xt1_ref)

    cur = ((qf_ref, kf_ref, gfh_ref, gfl_ref), (qb_ref, kb_ref, gbh_ref, gbl_ref))
    nxt = ((nqf_ref, nkf_ref, ngfh_ref, ngfl_ref), (nqb_ref, nkb_ref, ngbh_ref, ngbl_ref))
    vo = ((vf_ref, of_ref), (vb_ref, ob_ref))
    zero = jnp.zeros((c, LANES), BF16)

    def chunk_rows(j, d):
        jj = j if d == 0 else nchunk - 1 - j
        return pl.ds(pl.multiple_of(jj * c, c), c)

    def cur_src(j):
        return lambda d: (cur[d], chunk_rows(j, d))

    def nxt_src(jn):
        return lambda d: (nxt[d], pl.ds((jn if d == 0 else HG_AHEAD - 1 - jn) * c, c))

    def run_sums(src, slot):
        for d in range(2):
            (_, _, gh_ref, gl_ref), rows = src(d)
            for hp in range(HG_HEADS // 2):
                cols2 = slice(hp * 2 * LANES, (hp + 1) * 2 * LANES)
                g2 = jnp.concatenate([gh_ref[rows, cols2], gl_ref[rows, cols2]], axis=0)
                a_refs[slot][2 * d + hp] = _dot(run_ref[d], g2)

    def operands(src, slot, d, h, selq):
        (q_ref, k_ref, _, _), rows = src(d)
        cols = slice(h * LANES, (h + 1) * LANES)
        lanes = slice((h % 2) * LANES, (h % 2 + 1) * LANES)
        a_ref = a_refs[slot]
        ci = d * HG_HEADS + h
        kkb = k_ref[rows, cols]
        _hgrn_operands(d, a_ref[2 * d + h // 2, 0:c, lanes], a_ref[2 * d + h // 2, c:, lanes],
                       q_ref[rows, cols].astype(F32), kkb.astype(F32), kkb, selq,
                       x_refs[slot].at[ci], xt_refs[slot].at[ci], e_refs[slot].at[ci])

    def level_pair(j, slot, d, h, i):
        q_ref = cur[d][0]
        rows = chunk_rows(j, d)
        cols = slice(h * LANES, (h + 1) * LANES)
        ci = d * HG_HEADS + h
        x_ref, xt_ref = x_refs[slot].at[ci], xt_refs[slot].at[ci]
        l2 = x_ref[2 * i + 1] if 2 * i + 1 < nlev else q_ref[rows, cols]
        g2 = _dot(jnp.concatenate([x_ref[2 * i], l2], axis=1),
                  jnp.concatenate([jnp.concatenate([xt_ref[2 * i], zero], axis=1),
                                   jnp.concatenate([zero, xt_ref[2 * i + 1]], axis=1)], axis=0))
        g2 = g2.astype(BF16) * msk_ref[d, i]
        return g2[:, :c] + g2[:, c:]

    def finish(j, slot, d, h, p):
        v_ref, o_ref = vo[d]
        rows = chunk_rows(j, d)
        cols = slice(h * LANES, (h + 1) * LANES)
        ci = d * HG_HEADS + h
        vb = v_ref[rows, cols]
        st = st_ref[d, h]
        o_ref[rows, cols] = _dot(jnp.concatenate([p, x_refs[slot][ci, nlev]], axis=1),
                                 jnp.concatenate([vb, st.astype(BF16)], axis=0))
        e_col = jnp.transpose(jnp.broadcast_to(e_refs[slot][ci, 0:1, :], (c, LANES)))
        st_ref[d, h] = st * e_col + _dot(xt_refs[slot][ci, nlev + 1], vb)

    def sel_masks():
        return [[sel_ref[d, lf] != 0.0 for lf in range(len(HG_FINE))] for d in range(2)]

    chains = [(d, h) for d in range(2) for h in range(HG_HEADS)]
    assert len(chains) % npair == 0

    def step(j, slot, ahead2, ahead1):
        run_sums(ahead2, slot)
        selq = sel_masks()
        ps = [None] * len(chains)
        per = len(chains) // npair
        for i in range(npair):
            for ci, (d, h) in enumerate(chains):
                g = level_pair(j, slot, d, h, i)
                ps[ci] = g if ps[ci] is None else ps[ci] + g
            for d, h in chains[i * per:(i + 1) * per]:
                operands(ahead1, 1 - slot, d, h, selq[d])
        for ci, (d, h) in enumerate(chains):
            finish(j, slot, d, h, ps[ci])

    def body(i, carry):
        step(2 * i, 0, cur_src(2 * i + 2), cur_src(2 * i + 1))
        step(2 * i + 1, 1, cur_src(2 * i + 3), cur_src(2 * i + 2))
        return carry

    @pl.when(pl.program_id(1) == 0)
    def _():
        st_ref[...] = jnp.zeros_like(st_ref)
        run_sums(cur_src(0), 0)
        run_sums(cur_src(1), 1)
        selq = sel_masks()
        for d, h in chains:
            operands(cur_src(0), 0, d, h, selq[d])

    lax.fori_loop(0, nchunk // 2 - 1, body, 0)
    step(nchunk - 2, 0, nxt_src(0), cur_src(nchunk - 1))
    step(nchunk - 1, 1, nxt_src(1), nxt_src(0))


def _hgrn(p, consts, *, seq):
    n = p["q"].shape[0]
    nb, nt = n // seq, seq // HG_R
    run, msk, sel = consts
    ahead = HG_AHEAD * HG_C
    per = HG_R // ahead
    fwd = pl.BlockSpec((HG_R, HG_WIDTH), lambda b, t: (b * nt + t, 0))
    bwd = pl.BlockSpec((HG_R, HG_WIDTH), lambda b, t: (b * nt + nt - 1 - t, 0))
    nfwd = pl.BlockSpec((ahead, HG_WIDTH), lambda b, t: ((b * nt + jnp.minimum(t + 1, nt - 1)) * per, 0))
    nbwd = pl.BlockSpec((ahead, HG_WIDTH),
                        lambda b, t: ((b * nt + jnp.maximum(nt - 2 - t, 0)) * per + per - 1, 0))
    out = jax.ShapeDtypeStruct((n, HG_WIDTH), F32)
    return pl.pallas_call(
        _hgrn_kernel,
        grid=(nb, nt),
        in_specs=[fwd] * 5 + [bwd] * 5 + [nfwd] * 4 + [nbwd] * 4
        + [_const_spec(run.shape), _const_spec(msk.shape), _const_spec(sel.shape)],
        out_specs=[fwd, bwd],
        out_shape=[out, out],
        scratch_shapes=[pltpu.VMEM((2, HG_HEADS, HG_DK, LANES), F32)]
        + [pltpu.VMEM((HG_GROUPS, run.shape[1], 2 * LANES), F32)] * 2
        + [pltpu.VMEM((HG_CHAINS, len(HG_LEVELS) + 1, HG_C, LANES), BF16)] * 2
        + [pltpu.VMEM((HG_CHAINS, len(HG_LEVELS) + 2, HG_DK, HG_C), BF16)] * 2
        + [pltpu.VMEM((HG_CHAINS, SUBLANES, LANES), F32)] * 2,
        compiler_params=_params("arbitrary", "arbitrary"),
        name="hgrn2",
    )(p["q"], p["kf"], p["gfh"], p["gfl"], p["v"], p["q"], p["kb"], p["gbh"], p["gbl"], p["v"],
      p["q"], p["kf"], p["gfh"], p["gfl"], p["q"], p["kb"], p["gbh"], p["gbl"], run, msk, sel)


def _na_kernel(q_ref, k_ref, v_ref, bias_ref, o_ref, s0_ref, s1_ref, *, seq):
    rows = seq // GRID_W
    win = WIN_ROWS * GRID_W
    nit = rows // NA_ROWS_PER_ITER
    assert nit % 2 == 0
    s_refs = (s0_ref, s1_ref)
    lane = lax.broadcasted_iota(jnp.int32, (GRID_W, LANES), 1)
    first = lane < NA_DH

    def geometry(i):
        rs = [i * NA_ROWS_PER_ITER + u for u in range(NA_ROWS_PER_ITER)]
        r0s = [jnp.clip(r - WIN_ROWS // 2, 0, rows - WIN_ROWS) for r in rs]
        qrows = [pl.ds(pl.multiple_of(r * GRID_W, GRID_W), GRID_W) for r in rs]
        wrows = [pl.ds(pl.multiple_of(r0 * GRID_W, GRID_W), win) for r0 in r0s]
        return rs, r0s, qrows, wrows

    def scores(i, slot):
        rs, r0s, qrows, wrows = geometry(i)
        for u, (r, r0, qr_, wr_) in enumerate(zip(rs, r0s, qrows, wrows)):
            qr = q_ref[qr_, :]
            zero = jnp.zeros_like(qr)
            q2 = jnp.concatenate([jnp.where(first, qr, zero), jnp.where(first, zero, qr)], axis=0)
            s_refs[slot][u] = _dot_nt(q2, k_ref[wr_, :]) + bias_ref[0, r - r0]

    def attend(i, slot):
        _, _, qrows, wrows = geometry(i)
        ps, ls = [], []
        for u in range(NA_ROWS_PER_ITER):
            s = s_refs[slot][u]
            p = jnp.exp2(s - jnp.max(s, axis=-1, keepdims=True))
            ls.append(jnp.sum(p, axis=-1, keepdims=True))
            ps.append(p.astype(BF16))
        for p, l, qr_, wr_ in zip(ps, ls, qrows, wrows):
            o2 = _dot(p, v_ref[wr_, :]) * (1.0 / l)
            o_ref[qr_, :] = jnp.where(first, o2[:GRID_W], o2[GRID_W:]).astype(BF16)

    def body(j, carry):
        scores(2 * j + 1, 1)
        attend(2 * j, 0)
        scores(2 * j + 2, 0)
        attend(2 * j + 1, 1)
        return carry

    scores(0, 0)
    lax.fori_loop(0, nit // 2 - 1, body, 0)
    scores(nit - 1, 1)
    attend(nit - 2, 0)
    attend(nit - 1, 1)


def _na_bias_tables(na_rpb):
    qcol = np.arange(GRID_W)
    c0 = np.clip(qcol - WIN_COLS // 2, 0, GRID_W - WIN_COLS)
    col_mask = (qcol[None, :] >= c0[:, None]) & (qcol[None, :] < c0[:, None] + WIN_COLS)
    rel_c = np.clip(qcol[None, :] - qcol[:, None], -(WIN_COLS - 1), WIN_COLS - 1) + WIN_COLS - 1
    onehot = (rel_c[:, :, None] == np.arange(2 * WIN_COLS - 1)).astype(np.float32)
    t = jnp.einsum("qkc,lhrc->lhrqk", onehot, na_rpb.astype(F32), precision=lax.Precision.HIGHEST)
    t = jnp.where(col_mask, t * LOG2E, NEG_BIG)
    per_off = [t[:, :, WIN_ROWS - 1 - dl:2 * WIN_ROWS - 1 - dl] for dl in range(WIN_ROWS)]
    b = jnp.stack(per_off, axis=2)
    b = b.transpose(0, 1, 2, 4, 3, 5).reshape(DEPTH, NA_HEADS // 2, 2, WIN_ROWS, GRID_W, WIN_ROWS * GRID_W)
    return b.transpose(0, 1, 3, 2, 4, 5).reshape(DEPTH, NA_HEADS // 2, WIN_ROWS, 2 * GRID_W, WIN_ROWS * GRID_W)


def _na(p, w, l, *, seq):
    n = p["nq"].shape[0]
    nb = n // seq
    col = pl.BlockSpec((seq, LANES), lambda hp, b: (b, hp))
    return pl.pallas_call(
        functools.partial(_na_kernel, seq=seq),
        grid=(NA_HEADS // 2, nb),
        in_specs=[col, col, col,
                  pl.BlockSpec((None, 1, WIN_ROWS, 2 * GRID_W, WIN_ROWS * GRID_W),
                               lambda hp, b: (l, hp, 0, 0, 0))],
        out_specs=col,
        out_shape=jax.ShapeDtypeStruct((n, NA_WIDTH), BF16),
        scratch_shapes=[pltpu.VMEM((NA_ROWS_PER_ITER, 2 * GRID_W, WIN_ROWS * GRID_W), F32)] * 2,
        compiler_params=_params("arbitrary", "arbitrary"),
        name="natten",
    )(p["nq"], p["nk"], p["nv"], w["bias"])


def _lower_bounds(p):
    sm = jax.nn.softmax(p.astype(F32), axis=0)
    return jnp.cumsum(sm, axis=0) - sm[0]


def _trunk(x, b0, w, consts):
    nb, seq, _ = x.shape
    x = x.reshape(nb * seq, D_MODEL)
    for l in range(DEPTH):
        x = _ffn(x, w, l, 0, seq=seq, b0=b0)
        p = _in_proj(x, w, l, seq=seq, b0=b0)
        of, ob = _hgrn(p, consts, seq=seq)
        ona = _na(p, w, l, seq=seq)
        x = _ffn(x, w, l, 1, seq=seq, b0=b0, mix=(of, ob, p["sg"], ona))
    return x.reshape(nb, seq, D_MODEL)


def kernel(x_prompt, x_sample, c_prompt, c_sample, w_mod, b_mod, norm_g, ffn_w_gate, ffn_w_up, ffn_w_down,
           w_in, w_out, hg_lb_fwd, hg_lb_bwd, hg_norm_g, na_q_norm_g, na_k_norm_g, na_rpb):
    nbp = x_prompt.shape[0]
    c_all = jnp.concatenate([c_prompt, c_sample], axis=0)
    per_head = lambda g, scale: jnp.tile(g.astype(F32) * scale, (1, NA_HEADS)).reshape(DEPTH, 1, NA_WIDTH)
    w = {
        "mod": _modulation(c_all, w_mod, b_mod).reshape(DEPTH, c_all.shape[0], 9, D_MODEL),
        "norm_g": norm_g.reshape(DEPTH, 3, 1, D_MODEL),
        "wg": ffn_w_gate.astype(BF16), "wu": ffn_w_up.astype(BF16), "wd": ffn_w_down.astype(BF16),
        "w_in": w_in.astype(BF16), "w_out": w_out.astype(BF16),
        "lb_f": _lower_bounds(hg_lb_fwd).reshape(DEPTH, 1, HG_WIDTH),
        "lb_b": _lower_bounds(hg_lb_bwd).reshape(DEPTH, 1, HG_WIDTH),
        "hg_norm_g": hg_norm_g.reshape(DEPTH, 1, LANES),
        "qg": per_head(na_q_norm_g, NA_DH ** -0.5 * LOG2E), "kg": per_head(na_k_norm_g, 1.0),
        "bias": _na_bias_tables(na_rpb),
    }
    consts = _hgrn_constants()
    y_prompt = _trunk(x_prompt, 0, w, consts)
    y_sample = _trunk(x_sample, nbp, w, consts)
    return (y_prompt, y_sample)
```

```python
import functools
import math

import numpy as np
import jax
import jax.numpy as jnp
from jax import lax
from jax.experimental import pallas as pl
from jax.experimental.pallas import tpu as pltpu

F32 = jnp.float32
BF16 = jnp.bfloat16

D_MODEL = 1024
DEPTH = 4
GRID_W = 64
HG_HEADS = 4
HG_DK = 128
HG_WIDTH = 512
NA_HEADS = 8
NA_DH = 64
NA_WIDTH = 512
WIN_ROWS = 8
WIN_COLS = 16
D_FF = 2816
IN_WIDTH = 4096
EPS = 1e-6
LOG2E = math.log2(math.e)

LANES = 128
SUBLANES = 8
TM = 512
TM_PROJ = 1024
HG_C = 128
HG_R = 1024
HG_AHEAD = 2
HG_LEVELS = tuple(HG_C >> i for i in range(HG_C.bit_length() - 1))
HG_COARSE = tuple(s for s in HG_LEVELS if s // 2 >= SUBLANES)
HG_FINE = tuple(s for s in HG_LEVELS if s // 2 < SUBLANES)
HG_GROUPS = 2 * (HG_HEADS // 2)
HG_CHAINS = 2 * HG_HEADS
NA_ROWS_PER_ITER = 8
NEG_BIG = -1e30
VMEM_LIMIT = 56 * 1024 * 1024

NT_DIMS = (((1,), (1,)), ((), ()))
TN_DIMS = (((0,), (0,)), ((), ()))


def _dot(a, b):
    return jnp.dot(a, b, preferred_element_type=F32)


def _dot_nt(a, b):
    return lax.dot_general(a, b, NT_DIMS, preferred_element_type=F32)


def _sigmoid(x):
    return 1.0 / (1.0 + jnp.exp(-x))


def _silu(x):
    return x * _sigmoid(x)


def _layer_spec(shape, *lead):
    tail = (0,) * len(shape)
    return pl.BlockSpec((None,) * len(lead) + tuple(shape), lambda *_: tuple(lead) + tail,
                        pipeline_mode=pl.Buffered(1))


def _const_spec(shape):
    return _layer_spec(shape)


def _params(*sem):
    return pltpu.CompilerParams(dimension_semantics=sem, vmem_limit_bytes=VMEM_LIMIT)


def _mod_kernel(c_ref, w_ref, b_ref, o_ref):
    a = _silu(c_ref[...]).astype(BF16)
    o_ref[0] = _dot(a, w_ref[0].astype(BF16)) + b_ref[0]


def _modulation(c_all, w_mod, b_mod):
    nb = c_all.shape[0]
    tn = 1152
    return pl.pallas_call(
        _mod_kernel,
        grid=(DEPTH, 9 * D_MODEL // tn),
        in_specs=[
            pl.BlockSpec((nb, D_MODEL), lambda l, j: (0, 0)),
            pl.BlockSpec((1, D_MODEL, tn), lambda l, j: (l, 0, j)),
            pl.BlockSpec((1, 1, tn), lambda l, j: (l, 0, j)),
        ],
        out_specs=pl.BlockSpec((1, nb, tn), lambda l, j: (l, 0, j)),
        out_shape=jax.ShapeDtypeStruct((DEPTH, nb, 9 * D_MODEL), F32),
        compiler_params=_params("arbitrary", "arbitrary"),
        name="modulation",
    )(c_all, w_mod, b_mod.reshape(DEPTH, 1, 9 * D_MODEL))


def _norm_modulate(x, g, shift, scale):
    ms = jnp.mean(x * x, axis=-1, keepdims=True)
    h = (x * lax.rsqrt(ms + EPS)) * g
    return h * (1.0 + scale) + shift


def _mod_spec(l, b0, seq, tm):
    return pl.BlockSpec((None, 1, 9, D_MODEL), lambda i: (l, b0 + i * tm // seq, 0, 0))


def _ffn_kernel(*refs, sub, with_mix):
    if with_mix:
        (x_ref, of_ref, ob_ref, sg_ref, ona_ref, hng_ref, wout_ref,
         mod_ref, ng_ref, wg_ref, wu_ref, wd_ref, o_ref) = refs
    else:
        x_ref, mod_ref, ng_ref, wg_ref, wu_ref, wd_ref, o_ref = refs
    x = x_ref[...]
    if with_mix:
        heads = []
        for h in range(HG_HEADS):
            cols = slice(h * LANES, (h + 1) * LANES)
            o = of_ref[:, cols] + ob_ref[:, cols]
            ms = jnp.mean(o * o, axis=-1, keepdims=True)
            o = (o * lax.rsqrt(ms + EPS)) * hng_ref[...]
            heads.append((o * sg_ref[:, cols].astype(F32)).astype(BF16))
        ohg = jnp.concatenate(heads, axis=-1)
        mix = _dot(ohg, wout_ref[0:HG_WIDTH, :]) + _dot(ona_ref[...], wout_ref[HG_WIDTH:, :])
        x = x + mod_ref[0, 5:6, :] * mix
    shift = mod_ref[0, 3 * sub:3 * sub + 1, :]
    scale = mod_ref[0, 3 * sub + 1:3 * sub + 2, :]
    gate = mod_ref[0, 3 * sub + 2:3 * sub + 3, :]
    h = _norm_modulate(x, ng_ref[...], shift, scale).astype(BF16)
    g = _dot(h, wg_ref[...])
    u = _dot(h, wu_ref[...])
    a = (_silu(g) * u).astype(BF16)
    y = _dot(a, wd_ref[...])
    o_ref[...] = x + 0.5 * gate * y


def _ffn(x, w, l, ffn_idx, *, seq, b0, mix=None):
    n = x.shape[0]
    sub = 2 * ffn_idx
    tok = lambda width: pl.BlockSpec((TM, width), lambda i: (i, 0))
    args, specs = [x], [tok(D_MODEL)]
    if mix is not None:
        of, ob, sg, ona = mix
        args += [of, ob, sg, ona, w["hg_norm_g"], w["w_out"]]
        specs += [tok(HG_WIDTH), tok(HG_WIDTH), tok(HG_WIDTH), tok(NA_WIDTH),
                  _layer_spec((1, LANES), l), _layer_spec((D_MODEL, D_MODEL), l)]
    args += [w["mod"], w["norm_g"], w["wg"], w["wu"], w["wd"]]
    specs += [_mod_spec(l, b0, seq, TM), _layer_spec((1, D_MODEL), l, sub),
              _layer_spec((D_MODEL, D_FF), l, ffn_idx), _layer_spec((D_MODEL, D_FF), l, ffn_idx),
              _layer_spec((D_FF, D_MODEL), l, ffn_idx)]
    return pl.pallas_call(
        functools.partial(_ffn_kernel, sub=sub, with_mix=mix is not None),
        grid=(n // TM,),
        in_specs=specs,
        out_specs=tok(D_MODEL),
        out_shape=jax.ShapeDtypeStruct((n, D_MODEL), F32),
        compiler_params=_params("arbitrary"),
        name="ffn_mix" if mix is not None else "ffn",
    )(*args)


PROJ_OUTS = ("q", "kf", "kb", "gfh", "gfl", "gbh", "gbl", "v", "sg", "nq", "nk", "nv")


def _proj_kernel(x_ref, mod_ref, ng_ref, w_ref, lbf_ref, lbb_ref, qg_ref, kg_ref, *outs):
    o = dict(zip(PROJ_OUTS, outs))
    h = _norm_modulate(x_ref[...], ng_ref[...], mod_ref[0, 3:4, :], mod_ref[0, 4:5, :]).astype(BF16)
    seg = lambda i: _dot(h, w_ref[:, i * HG_WIDTH:(i + 1) * HG_WIDTH])

    def decay(i, lb_ref, k_ref, hi_ref, lo_ref):
        lb = lb_ref[...]
        f = lb + (1.0 - lb) * _sigmoid(seg(i))
        k_ref[...] = (1.0 - f).astype(BF16)
        g = jnp.log2(f)
        hi = g.astype(BF16)
        hi_ref[...] = hi
        lo_ref[...] = (g - hi.astype(F32)).astype(BF16)

    lane = lax.broadcasted_iota(jnp.int32, (x_ref.shape[0], LANES), 1)
    first = lane < NA_DH

    def head_norm(i, g_ref, o_ref):
        x = seg(i)
        for t in range(NA_WIDTH // LANES):
            cols = slice(t * LANES, (t + 1) * LANES)
            xt = x[:, cols]
            x2 = xt * xt
            sa = jnp.sum(jnp.where(first, x2, 0.0), axis=-1, keepdims=True)
            sb = jnp.sum(jnp.where(first, 0.0, x2), axis=-1, keepdims=True)
            ms = jnp.where(first, sa, sb) * (1.0 / NA_DH)
            o_ref[:, cols] = ((xt * lax.rsqrt(ms + EPS)) * g_ref[:, cols]).astype(BF16)

    decay(1, lbf_ref, o["kf"], o["gfh"], o["gfl"])
    decay(2, lbb_ref, o["kb"], o["gbh"], o["gbl"])
    head_norm(5, qg_ref, o["nq"])
    head_norm(6, kg_ref, o["nk"])
    o["q"][...] = _silu(seg(0)).astype(BF16)
    o["sg"][...] = _silu(seg(4)).astype(BF16)
    o["v"][...] = seg(3).astype(BF16)
    o["nv"][...] = seg(7).astype(BF16)


def _in_proj(x, w, l, *, seq, b0):
    n = x.shape[0]
    vec = _layer_spec((1, HG_WIDTH), l)
    outs = pl.pallas_call(
        _proj_kernel,
        grid=(n // TM_PROJ,),
        in_specs=[
            pl.BlockSpec((TM_PROJ, D_MODEL), lambda i: (i, 0)),
            _mod_spec(l, b0, seq, TM_PROJ),
            _layer_spec((1, D_MODEL), l, 1),
            _layer_spec((D_MODEL, IN_WIDTH), l),
            vec, vec, vec, vec,
        ],
        out_specs=[pl.BlockSpec((TM_PROJ, HG_WIDTH), lambda i: (i, 0))] * len(PROJ_OUTS),
        out_shape=[jax.ShapeDtypeStruct((n, HG_WIDTH), BF16)] * len(PROJ_OUTS),
        compiler_params=_params("arbitrary"),
        name="in_proj",
    )(x, w["mod"], w["norm_g"], w["w_in"], w["lb_f"], w["lb_b"], w["qg"], w["kg"])
    return dict(zip(PROJ_OUTS, outs))


def _hgrn_constants():
    c = HG_C
    idx = np.arange(c)
    t = idx[None, :]
    i = idx[:, None]
    run_blocks = [t <= i]
    sels, masks = [], []
    for s in HG_LEVELS:
        half = s // 2
        p = idx % s
        start = idx - p
        m = (start + half - 1)[:, None]
        is_q = p >= half
        masks.append(is_q[:, None] & ~is_q[None, :] & (start[:, None] == start[None, :]))
        if s in HG_FINE:
            qrow = (t >= m + 1) & (t <= i)
            krow = (t >= i + 1) & (t <= m)
            run_blocks.append(np.where(is_q[:, None], qrow, krow))
            sels.append(np.broadcast_to(is_q[:, None], (c, LANES)))
    masks.append(np.eye(c, dtype=bool))
    assert len(masks) % 2 == 0
    both = lambda a: np.stack([a, a[:, ::-1, ::-1]])
    run = both(np.stack(run_blocks).astype(np.float32))
    run = np.concatenate([run, run], axis=-1).reshape(2, -1, 2 * c)
    msk = both(np.stack(masks).astype(np.float32))
    msk = msk.reshape(2, len(masks) // 2, 2, c, c).transpose(0, 1, 3, 2, 4).reshape(2, -1, c, 2 * c)
    sel = np.stack(sels).astype(np.float32)
    sel = np.stack([sel, sel[:, ::-1, :]])
    return jnp.asarray(run, dtype=BF16), jnp.asarray(msk, dtype=BF16), jnp.asarray(sel, dtype=F32)


def _hgrn_operands(d, b, a_fine, q, kk, kkb, selq, x_ref, xt_ref, e_ref):
    c = HG_C
    nlev = len(HG_LEVELS)

    def put(l, x):
        xb = x.astype(BF16)
        x_ref[l] = xb
        xt_ref[l] = xb.T

    for l, s in enumerate(HG_COARSE):
        half = s // 2
        parts = []
        for blk in range(c // half):
            rows = slice(blk * half, (blk + 1) * half)
            base = (blk // 2) * s
            if d == 0:
                m, is_q = base + half - 1, blk % 2 == 1
            else:
                m, is_q = base + half, blk % 2 == 0
            bm = b[m:m + 1]
            if is_q:
                parts.append(q[rows] * jnp.exp2(b[rows] - bm))
            else:
                parts.append(kk[rows] * jnp.exp2(bm - b[rows]))
        put(l, jnp.concatenate(parts, axis=0))
    for lf in range(len(HG_FINE)):
        e = jnp.exp2(a_fine[lf * c:(lf + 1) * c])
        put(len(HG_COARSE) + lf, jnp.where(selq[lf], q, kk) * e)
    last = c - 1 if d == 0 else 0
    b_tot = b[last:last + 1]
    x_ref[nlev] = (q * jnp.exp2(b)).astype(BF16)
    xt_ref[nlev] = kkb.T
    xt_ref[nlev + 1] = (kk * jnp.exp2(b_tot - b)).astype(BF16).T
    e_ref[...] = jnp.broadcast_to(jnp.exp2(b_tot), e_ref.shape)


def _hgrn_kernel(qf_ref, kf_ref, gfh_ref, gfl_ref, vf_ref, qb_ref, kb_ref, gbh_ref, gbl_ref, vb_ref,
                 nqf_ref, nkf_ref, ngfh_ref, ngfl_ref, nqb_ref, nkb_ref, ngbh_ref, ngbl_ref,
                 run_ref, msk_ref, sel_ref, of_ref, ob_ref, st_ref,
                 a0_ref, a1_ref, x0_ref, x1_ref, xt0_ref, xt1_ref, e0_ref, e1_ref):
    c = HG_C
    nchunk = HG_R // c
    assert nchunk % 2 == 0 and nchunk >= 2 * HG_AHEAD
    nlev = len(HG_LEVELS)
    npair = (nlev + 1) // 2
    a_refs, x_refs, e_refs = (a0_ref, a1_ref), (x0_ref, x1_ref), (e0_ref, e1_ref)
    xt_refs = (xt0_ref, xt1_ref)

    cur = ((qf_ref, kf_ref, gfh_ref, gfl_ref), (qb_ref, kb_ref, gbh_ref, gbl_ref))
    nxt = ((nqf_ref, nkf_ref, ngfh_ref, ngfl_ref), (nqb_ref, nkb_ref, ngbh_ref, ngbl_ref))
    vo = ((vf_ref, of_ref), (vb_ref, ob_ref))
    zero = jnp.zeros((c, LANES), BF16)

    def chunk_rows(j, d):
        jj = j if d == 0 else nchunk - 1 - j
        return pl.ds(pl.multiple_of(jj * c, c), c)

    def cur_src(j):
        return lambda d: (cur[d], chunk_rows(j, d))

    def nxt_src(jn):
        return lambda d: (nxt[d], pl.ds((jn if d == 0 else HG_AHEAD - 1 - jn) * c, c))

    def run_sums(src, slot):
        for d in range(2):
            (_, _, gh_ref, gl_ref), rows = src(d)
            for hp in range(HG_HEADS // 2):
                cols2 = slice(hp * 2 * LANES, (hp + 1) * 2 * LANES)
                g2 = jnp.concatenate([gh_ref[rows, cols2], gl_ref[rows, cols2]], axis=0)
                a_refs[slot][2 * d + hp] = _dot(run_ref[d], g2)

    def operands(src, slot, d, h, selq):
        (q_ref, k_ref, _, _), rows = src(d)
        cols = slice(h * LANES, (h + 1) * LANES)
        lanes = slice((h % 2) * LANES, (h % 2 + 1) * LANES)
        a_ref = a_refs[slot]
        ci = d * HG_HEADS + h
        kkb = k_ref[rows, cols]
        _hgrn_operands(d, a_ref[2 * d + h // 2, 0:c, lanes], a_ref[2 * d + h // 2, c:, lanes],
                       q_ref[rows, cols].astype(F32), kkb.astype(F32), kkb, selq,
                       x_refs[slot].at[ci], xt_refs[slot].at[ci], e_refs[slot].at[ci])

    def level_pair(j, slot, d, h, i):
        q_ref = cur[d][0]
        rows = chunk_rows(j, d)
        cols = slice(h * LANES, (h + 1) * LANES)
        ci = d * HG_HEADS + h
        x_ref, xt_ref = x_refs[slot].at[ci], xt_refs[slot].at[ci]
        l2 = x_ref[2 * i + 1] if 2 * i + 1 < nlev else q_ref[rows, cols]
        g2 = _dot(jnp.concatenate([x_ref[2 * i], l2], axis=1),
                  jnp.concatenate([jnp.concatenate([xt_ref[2 * i], zero], axis=1),
                                   jnp.concatenate([zero, xt_ref[2 * i + 1]], axis=1)], axis=0))
        g2 = g2.astype(BF16) * msk_ref[d, i]
        return g2[:, :c] + g2[:, c:]

    def finish(j, slot, d, h, p):
        v_ref, o_ref = vo[d]
        rows = chunk_rows(j, d)
        cols = slice(h * LANES, (h + 1) * LANES)
        ci = d * HG_HEADS + h
        vb = v_ref[rows, cols]
        st = st_ref[d, h]
        o_ref[rows, cols] = _dot(jnp.concatenate([p, x_refs[slot][ci, nlev]], axis=1),
                                 jnp.concatenate([vb, st.astype(BF16)], axis=0))
        e_col = jnp.transpose(jnp.broadcast_to(e_refs[slot][ci, 0:1, :], (c, LANES)))
        st_ref[d, h] = st * e_col + _dot(xt_refs[slot][ci, nlev + 1], vb)

    def sel_masks():
        return [[sel_ref[d, lf] != 0.0 for lf in range(len(HG_FINE))] for d in range(2)]

    chains = [(d, h) for d in range(2) for h in range(HG_HEADS)]
    assert len(chains) % npair == 0

    def step(j, slot, ahead2, ahead1):
        run_sums(ahead2, slot)
        selq = sel_masks()
        ps = [None] * len(chains)
        per = len(chains) // npair
        for i in range(npair):
            for ci, (d, h) in enumerate(chains):
                g = level_pair(j, slot, d, h, i)
                ps[ci] = g if ps[ci] is None else ps[ci] + g
            for d, h in chains[i * per:(i + 1) * per]:
                operands(ahead1, 1 - slot, d, h, selq[d])
        for ci, (d, h) in enumerate(chains):
            finish(j, slot, d, h, ps[ci])

    def body(i, carry):
        step(2 * i, 0, cur_src(2 * i + 2), cur_src(2 * i + 1))
        step(2 * i + 1, 1, cur_src(2 * i + 3), cur_src(2 * i + 2))
        return carry

    @pl.when(pl.program_id(1) == 0)
    def _():
        st_ref[...] = jnp.zeros_like(st_ref)

    @pl.when(jnp.logical_and(pl.program_id(0) == 0, pl.program_id(1) == 0))
    def _():
        run_sums(cur_src(0), 0)
        run_sums(cur_src(1), 1)
        selq = sel_masks()
        for d, h in chains:
            operands(cur_src(0), 0, d, h, selq[d])

    lax.fori_loop(0, nchunk // 2 - 1, body, 0)
    step(nchunk - 2, 0, nxt_src(0), cur_src(nchunk - 1))
    step(nchunk - 1, 1, nxt_src(1), nxt_src(0))


def _hgrn(p, consts, *, seq):
    n = p["q"].shape[0]
    nb, nt = n // seq, seq // HG_R
    run, msk, sel = consts
    ahead = HG_AHEAD * HG_C
    per = HG_R // ahead
    fwd = pl.BlockSpec((HG_R, HG_WIDTH), lambda b, t: (b * nt + t, 0))
    bwd = pl.BlockSpec((HG_R, HG_WIDTH), lambda b, t: (b * nt + nt - 1 - t, 0))
    def next_step(b, t):
        s = jnp.minimum(b * nt + t + 1, nb * nt - 1)
        return s // nt, s % nt

    def nfwd_map(b, t):
        b1, t1 = next_step(b, t)
        return ((b1 * nt + t1) * per, 0)

    def nbwd_map(b, t):
        b1, t1 = next_step(b, t)
        return ((b1 * nt + nt - 1 - t1) * per + per - 1, 0)

    nfwd = pl.BlockSpec((ahead, HG_WIDTH), nfwd_map)
    nbwd = pl.BlockSpec((ahead, HG_WIDTH), nbwd_map)
    out = jax.ShapeDtypeStruct((n, HG_WIDTH), F32)
    return pl.pallas_call(
        _hgrn_kernel,
        grid=(nb, nt),
        in_specs=[fwd] * 5 + [bwd] * 5 + [nfwd] * 4 + [nbwd] * 4
        + [_const_spec(run.shape), _const_spec(msk.shape), _const_spec(sel.shape)],
        out_specs=[fwd, bwd],
        out_shape=[out, out],
        scratch_shapes=[pltpu.VMEM((2, HG_HEADS, HG_DK, LANES), F32)]
        + [pltpu.VMEM((HG_GROUPS, run.shape[1], 2 * LANES), F32)] * 2
        + [pltpu.VMEM((HG_CHAINS, len(HG_LEVELS) + 1, HG_C, LANES), BF16)] * 2
        + [pltpu.VMEM((HG_CHAINS, len(HG_LEVELS) + 2, HG_DK, HG_C), BF16)] * 2
        + [pltpu.VMEM((HG_CHAINS, SUBLANES, LANES), F32)] * 2,
        compiler_params=_params("arbitrary", "arbitrary"),
        name="hgrn2",
    )(p["q"], p["kf"], p["gfh"], p["gfl"], p["v"], p["q"], p["kb"], p["gbh"], p["gbl"], p["v"],
      p["q"], p["kf"], p["gfh"], p["gfl"], p["q"], p["kb"], p["gbh"], p["gbl"], run, msk, sel)


def _na_kernel(q_ref, k_ref, v_ref, bias_ref, o_ref, s0_ref, s1_ref, *, seq):
    rows = seq // GRID_W
    win = WIN_ROWS * GRID_W
    nit = rows // NA_ROWS_PER_ITER
    assert nit % 2 == 0
    s_refs = (s0_ref, s1_ref)
    lane = lax.broadcasted_iota(jnp.int32, (GRID_W, LANES), 1)
    first = lane < NA_DH

    def geometry(i):
        rs = [i * NA_ROWS_PER_ITER + u for u in range(NA_ROWS_PER_ITER)]
        r0s = [jnp.clip(r - WIN_ROWS // 2, 0, rows - WIN_ROWS) for r in rs]
        qrows = [pl.ds(pl.multiple_of(r * GRID_W, GRID_W), GRID_W) for r in rs]
        wrows = [pl.ds(pl.multiple_of(r0 * GRID_W, GRID_W), win) for r0 in r0s]
        return rs, r0s, qrows, wrows

    def scores(i, slot):
        rs, r0s, qrows, wrows = geometry(i)
        for u, (r, r0, qr_, wr_) in enumerate(zip(rs, r0s, qrows, wrows)):
            qr = q_ref[qr_, :]
            zero = jnp.zeros_like(qr)
            q2 = jnp.concatenate([jnp.where(first, qr, zero), jnp.where(first, zero, qr)], axis=0)
            rho0 = WIN_ROWS - 1 - (r - r0)
            bias = jnp.concatenate(
                [jnp.concatenate([bias_ref[0, a, rho0 + 2 * m] for m in range(WIN_ROWS // 2)], axis=1)
                 for a in range(2)], axis=0)
            s_refs[slot][u] = _dot_nt(q2, k_ref[wr_, :]) + bias

    def attend(i, slot):
        _, _, qrows, wrows = geometry(i)
        ps, ls = [], []
        for u in range(NA_ROWS_PER_ITER):
            s = s_refs[slot][u]
            p = jnp.exp2(s - jnp.max(s, axis=-1, keepdims=True))
            ls.append(jnp.sum(p, axis=-1, keepdims=True))
            ps.append(p.astype(BF16))
        for p, l, qr_, wr_ in zip(ps, ls, qrows, wrows):
            o2 = _dot(p, v_ref[wr_, :]) * (1.0 / l)
            o_ref[qr_, :] = jnp.where(first, o2[:GRID_W], o2[GRID_W:]).astype(BF16)

    def body(j, carry):
        scores(2 * j + 1, 1)
        attend(2 * j, 0)
        scores(2 * j + 2, 0)
        attend(2 * j + 1, 1)
        return carry

    scores(0, 0)
    lax.fori_loop(0, nit // 2 - 1, body, 0)
    scores(nit - 1, 1)
    attend(nit - 2, 0)
    attend(nit - 1, 1)


def _na_bias_tables(na_rpb):
    qcol = np.arange(GRID_W)
    c0 = np.clip(qcol - WIN_COLS // 2, 0, GRID_W - WIN_COLS)
    col_mask = (qcol[None, :] >= c0[:, None]) & (qcol[None, :] < c0[:, None] + WIN_COLS)
    rel_c = np.clip(qcol[None, :] - qcol[:, None], -(WIN_COLS - 1), WIN_COLS - 1) + WIN_COLS - 1
    onehot = (rel_c[:, :, None] == np.arange(2 * WIN_COLS - 1)).astype(np.float32)
    t = jnp.einsum("qkc,lhrc->lhrqk", onehot, na_rpb.astype(F32), precision=lax.Precision.HIGHEST)
    t = jnp.where(col_mask, t * LOG2E, NEG_BIG)
    pairs = jnp.concatenate([t[:, :, :-1], t[:, :, 1:]], axis=-1)
    return pairs.reshape(DEPTH, NA_HEADS // 2, 2, 2 * WIN_ROWS - 2, GRID_W, LANES)


def _na(p, w, l, *, seq):
    n = p["nq"].shape[0]
    nb = n // seq
    col = pl.BlockSpec((seq, LANES), lambda hp, b: (b, hp))
    return pl.pallas_call(
        functools.partial(_na_kernel, seq=seq),
        grid=(NA_HEADS // 2, nb),
        in_specs=[col, col, col,
                  pl.BlockSpec((None, 1, 2, 2 * WIN_ROWS - 2, GRID_W, LANES),
                               lambda hp, b: (l, hp, 0, 0, 0, 0))],
        out_specs=col,
        out_shape=jax.ShapeDtypeStruct((n, NA_WIDTH), BF16),
        scratch_shapes=[pltpu.VMEM((NA_ROWS_PER_ITER, 2 * GRID_W, WIN_ROWS * GRID_W), F32)] * 2,
        compiler_params=_params("arbitrary", "arbitrary"),
        name="natten",
    )(p["nq"], p["nk"], p["nv"], w["bias"])


def _lower_bounds(p):
    sm = jax.nn.softmax(p.astype(F32), axis=0)
    return jnp.cumsum(sm, axis=0) - sm[0]


def _trunk(x, b0, w, consts):
    nb, seq, _ = x.shape
    x = x.reshape(nb * seq, D_MODEL)
    for l in range(DEPTH):
        x = _ffn(x, w, l, 0, seq=seq, b0=b0)
        p = _in_proj(x, w, l, seq=seq, b0=b0)
        of, ob = _hgrn(p, consts, seq=seq)
        ona = _na(p, w, l, seq=seq)
        x = _ffn(x, w, l, 1, seq=seq, b0=b0, mix=(of, ob, p["sg"], ona))
    return x.reshape(nb, seq, D_MODEL)


def kernel(x_prompt, x_sample, c_prompt, c_sample, w_mod, b_mod, norm_g, ffn_w_gate, ffn_w_up, ffn_w_down,
           w_in, w_out, hg_lb_fwd, hg_lb_bwd, hg_norm_g, na_q_norm_g, na_k_norm_g, na_rpb):
    nbp = x_prompt.shape[0]
    c_all = jnp.concatenate([c_prompt, c_sample], axis=0)
    per_head = lambda g, scale: jnp.tile(g.astype(F32) * scale, (1, NA_HEADS)).reshape(DEPTH, 1, NA_WIDTH)
    w = {
        "mod": _modulation(c_all, w_mod, b_mod).reshape(DEPTH, c_all.shape[0], 9, D_MODEL),
        "norm_g": norm_g.reshape(DEPTH, 3, 1, D_MODEL),
        "wg": ffn_w_gate.astype(BF16), "wu": ffn_w_up.astype(BF16), "wd": ffn_w_down.astype(BF16),
        "w_in": w_in.astype(BF16), "w_out": w_out.astype(BF16),
        "lb_f": _lower_bounds(hg_lb_fwd).reshape(DEPTH, 1, HG_WIDTH),
        "lb_b": _lower_bounds(hg_lb_bwd).reshape(DEPTH, 1, HG_WIDTH),
        "hg_norm_g": hg_norm_g.reshape(DEPTH, 1, LANES),
        "qg": per_head(na_q_norm_g, NA_DH ** -0.5 * LOG2E), "kg": per_head(na_k_norm_g, 1.0),
        "bias": _na_bias_tables(na_rpb),
    }
    consts = _hgrn_constants()
    y_prompt = _trunk(x_prompt, 0, w, consts)
    y_sample = _trunk(x_sample, nbp, w, consts)
    return (y_prompt, y_sample)
```

```python
import functools
import math

import numpy as np
import jax
import jax.numpy as jnp
from jax import lax
from jax.experimental import pallas as pl
from jax.experimental.pallas import tpu as pltpu

F32 = jnp.float32
BF16 = jnp.bfloat16

D_MODEL = 1024
DEPTH = 4
GRID_W = 64
HG_HEADS = 4
HG_DK = 128
HG_WIDTH = 512
NA_HEADS = 8
NA_DH = 64
NA_WIDTH = 512
WIN_ROWS = 8
WIN_COLS = 16
D_FF = 2816
IN_WIDTH = 4096
EPS = 1e-6
LOG2E = math.log2(math.e)

LANES = 128
SUBLANES = 8
TM = 512
TM_PROJ = 1024
HG_C = 128
HG_R = 512
HG_AHEAD = 2
HG_LEVELS = tuple(HG_C >> i for i in range(HG_C.bit_length() - 1))
HG_COARSE = tuple(s for s in HG_LEVELS if s // 2 >= SUBLANES)
HG_FINE = tuple(s for s in HG_LEVELS if s // 2 < SUBLANES)
HG_GROUPS = 2 * (HG_HEADS // 2)
HG_CHAINS = 2 * HG_HEADS
NA_STEP_ROWS = (NA_HEADS // 2) * HG_R // GRID_W
NA_ROWS_PER_ITER = NA_STEP_ROWS // (HG_R // HG_C)
NEG_BIG = -1e30
VMEM_LIMIT = 56 * 1024 * 1024

NT_DIMS = (((1,), (1,)), ((), ()))
TN_DIMS = (((0,), (0,)), ((), ()))


def _dot(a, b):
    return jnp.dot(a, b, preferred_element_type=F32)


def _dot_nt(a, b):
    return lax.dot_general(a, b, NT_DIMS, preferred_element_type=F32)


def _sigmoid(x):
    return 1.0 / (1.0 + jnp.exp(-x))


def _silu(x):
    return x * _sigmoid(x)


def _layer_spec(shape, *lead):
    tail = (0,) * len(shape)
    return pl.BlockSpec((None,) * len(lead) + tuple(shape), lambda *_: tuple(lead) + tail,
                        pipeline_mode=pl.Buffered(1))


def _const_spec(shape):
    return _layer_spec(shape)


def _params(*sem):
    return pltpu.CompilerParams(dimension_semantics=sem, vmem_limit_bytes=VMEM_LIMIT)


def _mod_kernel(c_ref, w_ref, b_ref, o_ref):
    a = _silu(c_ref[...]).astype(BF16)
    o_ref[0] = _dot(a, w_ref[0].astype(BF16)) + b_ref[0]


def _modulation(c_all, w_mod, b_mod):
    nb = c_all.shape[0]
    tn = 1152
    return pl.pallas_call(
        _mod_kernel,
        grid=(DEPTH, 9 * D_MODEL // tn),
        in_specs=[
            pl.BlockSpec((nb, D_MODEL), lambda l, j: (0, 0)),
            pl.BlockSpec((1, D_MODEL, tn), lambda l, j: (l, 0, j)),
            pl.BlockSpec((1, 1, tn), lambda l, j: (l, 0, j)),
        ],
        out_specs=pl.BlockSpec((1, nb, tn), lambda l, j: (l, 0, j)),
        out_shape=jax.ShapeDtypeStruct((DEPTH, nb, 9 * D_MODEL), F32),
        compiler_params=_params("arbitrary", "arbitrary"),
        name="modulation",
    )(c_all, w_mod, b_mod.reshape(DEPTH, 1, 9 * D_MODEL))


def _norm_modulate(x, g, shift, scale):
    ms = jnp.mean(x * x, axis=-1, keepdims=True)
    h = (x * lax.rsqrt(ms + EPS)) * g
    return h * (1.0 + scale) + shift


def _mod_spec(l, b0, seq, tm):
    return pl.BlockSpec((None, 1, 9, D_MODEL), lambda i: (l, b0 + i * tm // seq, 0, 0))


def _ffn_kernel(*refs, sub, with_mix):
    if with_mix:
        (x_ref, of_ref, ob_ref, sg_ref, ona_ref, hng_ref, wout_ref,
         mod_ref, ng_ref, wg_ref, wu_ref, wd_ref, o_ref) = refs
    else:
        x_ref, mod_ref, ng_ref, wg_ref, wu_ref, wd_ref, o_ref = refs
    x = x_ref[...]
    if with_mix:
        heads = []
        for h in range(HG_HEADS):
            cols = slice(h * LANES, (h + 1) * LANES)
            o = of_ref[:, cols] + ob_ref[:, cols]
            ms = jnp.mean(o * o, axis=-1, keepdims=True)
            o = (o * lax.rsqrt(ms + EPS)) * hng_ref[...]
            heads.append((o * sg_ref[:, cols].astype(F32)).astype(BF16))
        ohg = jnp.concatenate(heads, axis=-1)
        mix = _dot(ohg, wout_ref[0:HG_WIDTH, :]) + _dot(ona_ref[...], wout_ref[HG_WIDTH:, :])
        x = x + mod_ref[0, 5:6, :] * mix
    shift = mod_ref[0, 3 * sub:3 * sub + 1, :]
    scale = mod_ref[0, 3 * sub + 1:3 * sub + 2, :]
    gate = mod_ref[0, 3 * sub + 2:3 * sub + 3, :]
    h = _norm_modulate(x, ng_ref[...], shift, scale).astype(BF16)
    g = _dot(h, wg_ref[...])
    u = _dot(h, wu_ref[...])
    a = (_silu(g) * u).astype(BF16)
    y = _dot(a, wd_ref[...])
    o_ref[...] = x + 0.5 * gate * y


def _ffn(x, w, l, ffn_idx, *, seq, b0, mix=None):
    n = x.shape[0]
    sub = 2 * ffn_idx
    tok = lambda width: pl.BlockSpec((TM, width), lambda i: (i, 0))
    args, specs = [x], [tok(D_MODEL)]
    if mix is not None:
        of, ob, sg, ona = mix
        args += [of, ob, sg, ona, w["hg_norm_g"], w["w_out"]]
        specs += [tok(HG_WIDTH), tok(HG_WIDTH), tok(HG_WIDTH), tok(NA_WIDTH),
                  _layer_spec((1, LANES), l), _layer_spec((D_MODEL, D_MODEL), l)]
    args += [w["mod"], w["norm_g"], w["wg"], w["wu"], w["wd"]]
    specs += [_mod_spec(l, b0, seq, TM), _layer_spec((1, D_MODEL), l, sub),
              _layer_spec((D_MODEL, D_FF), l, ffn_idx), _layer_spec((D_MODEL, D_FF), l, ffn_idx),
              _layer_spec((D_FF, D_MODEL), l, ffn_idx)]
    return pl.pallas_call(
        functools.partial(_ffn_kernel, sub=sub, with_mix=mix is not None),
        grid=(n // TM,),
        in_specs=specs,
        out_specs=tok(D_MODEL),
        out_shape=jax.ShapeDtypeStruct((n, D_MODEL), F32),
        compiler_params=_params("arbitrary"),
        name="ffn_mix" if mix is not None else "ffn",
    )(*args)


PROJ_OUTS = ("q", "kf", "kb", "gfh", "gfl", "gbh", "gbl", "v", "sg", "nq", "nk", "nv")


def _proj_kernel(x_ref, mod_ref, ng_ref, w_ref, lbf_ref, lbb_ref, qg_ref, kg_ref, *outs):
    o = dict(zip(PROJ_OUTS, outs))
    h = _norm_modulate(x_ref[...], ng_ref[...], mod_ref[0, 3:4, :], mod_ref[0, 4:5, :]).astype(BF16)
    seg = lambda i: _dot(h, w_ref[:, i * HG_WIDTH:(i + 1) * HG_WIDTH])

    def decay(i, lb_ref, k_ref, hi_ref, lo_ref):
        lb = lb_ref[...]
        f = lb + (1.0 - lb) * _sigmoid(seg(i))
        k_ref[...] = (1.0 - f).astype(BF16)
        g = jnp.log2(f)
        hi = g.astype(BF16)
        hi_ref[...] = hi
        lo_ref[...] = (g - hi.astype(F32)).astype(BF16)

    lane = lax.broadcasted_iota(jnp.int32, (x_ref.shape[0], LANES), 1)
    first = lane < NA_DH

    def head_norm(i, g_ref, o_ref):
        x = seg(i)
        for t in range(NA_WIDTH // LANES):
            cols = slice(t * LANES, (t + 1) * LANES)
            xt = x[:, cols]
            x2 = xt * xt
            sa = jnp.sum(jnp.where(first, x2, 0.0), axis=-1, keepdims=True)
            sb = jnp.sum(jnp.where(first, 0.0, x2), axis=-1, keepdims=True)
            ms = jnp.where(first, sa, sb) * (1.0 / NA_DH)
            o_ref[:, cols] = ((xt * lax.rsqrt(ms + EPS)) * g_ref[:, cols]).astype(BF16)

    decay(1, lbf_ref, o["kf"], o["gfh"], o["gfl"])
    decay(2, lbb_ref, o["kb"], o["gbh"], o["gbl"])
    head_norm(5, qg_ref, o["nq"])
    head_norm(6, kg_ref, o["nk"])
    o["q"][...] = _silu(seg(0)).astype(BF16)
    o["sg"][...] = _silu(seg(4)).astype(BF16)
    o["v"][...] = seg(3).astype(BF16)
    o["nv"][...] = seg(7).astype(BF16)


def _in_proj(x, w, l, *, seq, b0):
    n = x.shape[0]
    vec = _layer_spec((1, HG_WIDTH), l)
    outs = pl.pallas_call(
        _proj_kernel,
        grid=(n // TM_PROJ,),
        in_specs=[
            pl.BlockSpec((TM_PROJ, D_MODEL), lambda i: (i, 0)),
            _mod_spec(l, b0, seq, TM_PROJ),
            _layer_spec((1, D_MODEL), l, 1),
            _layer_spec((D_MODEL, IN_WIDTH), l),
            vec, vec, vec, vec,
        ],
        out_specs=[pl.BlockSpec((TM_PROJ, HG_WIDTH), lambda i: (i, 0))] * len(PROJ_OUTS),
        out_shape=[jax.ShapeDtypeStruct((n, HG_WIDTH), BF16)] * len(PROJ_OUTS),
        compiler_params=_params("arbitrary"),
        name="in_proj",
    )(x, w["mod"], w["norm_g"], w["w_in"], w["lb_f"], w["lb_b"], w["qg"], w["kg"])
    return dict(zip(PROJ_OUTS, outs))


def _hgrn_constants():
    c = HG_C
    idx = np.arange(c)
    t = idx[None, :]
    i = idx[:, None]
    run_blocks = [t <= i]
    sels, masks = [], []
    for s in HG_LEVELS:
        half = s // 2
        p = idx % s
        start = idx - p
        m = (start + half - 1)[:, None]
        is_q = p >= half
        masks.append(is_q[:, None] & ~is_q[None, :] & (start[:, None] == start[None, :]))
        if s in HG_FINE:
            qrow = (t >= m + 1) & (t <= i)
            krow = (t >= i + 1) & (t <= m)
            run_blocks.append(np.where(is_q[:, None], qrow, krow))
            sels.append(np.broadcast_to(is_q[:, None], (c, LANES)))
    masks.append(np.eye(c, dtype=bool))
    assert len(masks) % 2 == 0
    both = lambda a: np.stack([a, a[:, ::-1, ::-1]])
    run = both(np.stack(run_blocks).astype(np.float32))
    run = np.concatenate([run, run], axis=-1).reshape(2, -1, 2 * c)
    msk = both(np.stack(masks).astype(np.float32))
    msk = msk.reshape(2, len(masks) // 2, 2, c, c).transpose(0, 1, 3, 2, 4).reshape(2, -1, c, 2 * c)
    sel = np.stack(sels).astype(np.float32)
    sel = np.stack([sel, sel[:, ::-1, :]])
    return jnp.asarray(run, dtype=BF16), jnp.asarray(msk, dtype=BF16), jnp.asarray(sel, dtype=F32)


def _hgrn_operands(d, b, a_fine, q, kk, kkb, selq, x_ref, xt_ref, e_ref):
    c = HG_C
    nlev = len(HG_LEVELS)

    def put(l, x):
        xb = x.astype(BF16)
        x_ref[l] = xb
        xt_ref[l] = xb.T

    for l, s in enumerate(HG_COARSE):
        half = s // 2
        parts = []
        for blk in range(c // half):
            rows = slice(blk * half, (blk + 1) * half)
            base = (blk // 2) * s
            if d == 0:
                m, is_q = base + half - 1, blk % 2 == 1
            else:
                m, is_q = base + half, blk % 2 == 0
            bm = b[m:m + 1]
            if is_q:
                parts.append(q[rows] * jnp.exp2(b[rows] - bm))
            else:
                parts.append(kk[rows] * jnp.exp2(bm - b[rows]))
        put(l, jnp.concatenate(parts, axis=0))
    for lf in range(len(HG_FINE)):
        e = jnp.exp2(a_fine[lf * c:(lf + 1) * c])
        put(len(HG_COARSE) + lf, jnp.where(selq[lf], q, kk) * e)
    last = c - 1 if d == 0 else 0
    b_tot = b[last:last + 1]
    x_ref[nlev] = (q * jnp.exp2(b)).astype(BF16)
    xt_ref[nlev] = kkb.T
    xt_ref[nlev + 1] = (kk * jnp.exp2(b_tot - b)).astype(BF16).T
    e_ref[...] = jnp.broadcast_to(jnp.exp2(b_tot), e_ref.shape)


def _mixer_kernel(qf_ref, kf_ref, gfh_ref, gfl_ref, vf_ref, qb_ref, kb_ref, gbh_ref, gbl_ref, vb_ref,
                  nqf_ref, nkf_ref, ngfh_ref, ngfl_ref, nqb_ref, nkb_ref, ngbh_ref, ngbl_ref,
                  aq_ref, ak_ref, av_ref, bias_ref,
                  run_ref, msk_ref, sel_ref, of_ref, ob_ref, ona_ref, st_ref,
                  a0_ref, a1_ref, x0_ref, x1_ref, xt0_ref, xt1_ref, e0_ref, e1_ref, ns_ref, *, seq):
    c = HG_C
    nchunk = HG_R // c
    assert nchunk % 2 == 0 and nchunk >= 2 * HG_AHEAD
    na = _na_stage_fns(aq_ref, ak_ref, av_ref, bias_ref, ona_ref, ns_ref, seq)
    nlev = len(HG_LEVELS)
    npair = (nlev + 1) // 2
    a_refs, x_refs, e_refs = (a0_ref, a1_ref), (x0_ref, x1_ref), (e0_ref, e1_ref)
    xt_refs = (xt0_ref, xt1_ref)

    cur = ((qf_ref, kf_ref, gfh_ref, gfl_ref), (qb_ref, kb_ref, gbh_ref, gbl_ref))
    nxt = ((nqf_ref, nkf_ref, ngfh_ref, ngfl_ref), (nqb_ref, nkb_ref, ngbh_ref, ngbl_ref))
    vo = ((vf_ref, of_ref), (vb_ref, ob_ref))
    zero = jnp.zeros((c, LANES), BF16)

    def chunk_rows(j, d):
        jj = j if d == 0 else nchunk - 1 - j
        return pl.ds(pl.multiple_of(jj * c, c), c)

    def cur_src(j):
        return lambda d: (cur[d], chunk_rows(j, d))

    def nxt_src(jn):
        return lambda d: (nxt[d], pl.ds((jn if d == 0 else HG_AHEAD - 1 - jn) * c, c))

    def run_sums(src, slot):
        for d in range(2):
            (_, _, gh_ref, gl_ref), rows = src(d)
            for hp in range(HG_HEADS // 2):
                cols2 = slice(hp * 2 * LANES, (hp + 1) * 2 * LANES)
                g2 = jnp.concatenate([gh_ref[rows, cols2], gl_ref[rows, cols2]], axis=0)
                a_refs[slot][2 * d + hp] = _dot(run_ref[d], g2)

    def operands(src, slot, d, h, selq):
        (q_ref, k_ref, _, _), rows = src(d)
        cols = slice(h * LANES, (h + 1) * LANES)
        lanes = slice((h % 2) * LANES, (h % 2 + 1) * LANES)
        a_ref = a_refs[slot]
        ci = d * HG_HEADS + h
        kkb = k_ref[rows, cols]
        _hgrn_operands(d, a_ref[2 * d + h // 2, 0:c, lanes], a_ref[2 * d + h // 2, c:, lanes],
                       q_ref[rows, cols].astype(F32), kkb.astype(F32), kkb, selq,
                       x_refs[slot].at[ci], xt_refs[slot].at[ci], e_refs[slot].at[ci])

    def level_pair(j, slot, d, h, i):
        q_ref = cur[d][0]
        rows = chunk_rows(j, d)
        cols = slice(h * LANES, (h + 1) * LANES)
        ci = d * HG_HEADS + h
        x_ref, xt_ref = x_refs[slot].at[ci], xt_refs[slot].at[ci]
        l2 = x_ref[2 * i + 1] if 2 * i + 1 < nlev else q_ref[rows, cols]
        g2 = _dot(jnp.concatenate([x_ref[2 * i], l2], axis=1),
                  jnp.concatenate([jnp.concatenate([xt_ref[2 * i], zero], axis=1),
                                   jnp.concatenate([zero, xt_ref[2 * i + 1]], axis=1)], axis=0))
        g2 = g2.astype(BF16) * msk_ref[d, i]
        return g2[:, :c] + g2[:, c:]

    def finish(j, slot, d, h, p):
        v_ref, o_ref = vo[d]
        rows = chunk_rows(j, d)
        cols = slice(h * LANES, (h + 1) * LANES)
        ci = d * HG_HEADS + h
        vb = v_ref[rows, cols]
        st = st_ref[d, h]
        o_ref[rows, cols] = _dot(jnp.concatenate([p, x_refs[slot][ci, nlev]], axis=1),
                                 jnp.concatenate([vb, st.astype(BF16)], axis=0))
        e_col = jnp.transpose(jnp.broadcast_to(e_refs[slot][ci, 0:1, :], (c, LANES)))
        st_ref[d, h] = st * e_col + _dot(xt_refs[slot][ci, nlev + 1], vb)

    def sel_masks():
        return [[sel_ref[d, lf] != 0.0 for lf in range(len(HG_FINE))] for d in range(2)]

    chains = [(d, h) for d in range(2) for h in range(HG_HEADS)]
    assert len(chains) % npair == 0

    def step(j, slot, ahead2, ahead1):
        run_sums(ahead2, slot)
        na_scores, na_softmax, na_values = na
        na_scores(j)
        selq = sel_masks()
        ps = [None] * len(chains)
        per = len(chains) // npair
        probs = None
        for i in range(npair):
            for ci, (d, h) in enumerate(chains):
                g = level_pair(j, slot, d, h, i)
                ps[ci] = g if ps[ci] is None else ps[ci] + g
            for d, h in chains[i * per:(i + 1) * per]:
                operands(ahead1, 1 - slot, d, h, selq[d])
            if i == npair // 2 - 1:
                probs = na_softmax()
        na_values(j, probs)
        for ci, (d, h) in enumerate(chains):
            finish(j, slot, d, h, ps[ci])

    def body(i, carry):
        step(2 * i, 0, cur_src(2 * i + 2), cur_src(2 * i + 1))
        step(2 * i + 1, 1, cur_src(2 * i + 3), cur_src(2 * i + 2))
        return carry

    @pl.when(pl.program_id(1) == 0)
    def _():
        st_ref[...] = jnp.zeros_like(st_ref)

    @pl.when(jnp.logical_and(pl.program_id(0) == 0, pl.program_id(1) == 0))
    def _():
        run_sums(cur_src(0), 0)
        run_sums(cur_src(1), 1)
        selq = sel_masks()
        for d, h in chains:
            operands(cur_src(0), 0, d, h, selq[d])

    lax.fori_loop(0, nchunk // 2 - 1, body, 0)
    step(nchunk - 2, 0, nxt_src(0), cur_src(nchunk - 1))
    step(nchunk - 1, 1, nxt_src(1), nxt_src(0))


def _mixer(p, w, l, consts, *, seq):
    n = p["q"].shape[0]
    nb, nt = n // seq, seq // HG_R
    rows = seq // GRID_W
    assert nt * NA_STEP_ROWS == (NA_HEADS // 2) * rows and rows % NA_STEP_ROWS == 0
    run, msk, sel = consts
    head_pair = lambda t: (t * NA_STEP_ROWS) // rows
    col = pl.BlockSpec((seq, LANES), lambda b, t: (b, head_pair(t)))
    bias = pl.BlockSpec((None, 1, 2, 2 * WIN_ROWS - 2, GRID_W, LANES),
                        lambda b, t: (l, head_pair(t), 0, 0, 0, 0))
    ahead = HG_AHEAD * HG_C
    per = HG_R // ahead
    fwd = pl.BlockSpec((HG_R, HG_WIDTH), lambda b, t: (b * nt + t, 0))
    bwd = pl.BlockSpec((HG_R, HG_WIDTH), lambda b, t: (b * nt + nt - 1 - t, 0))
    def next_step(b, t):
        s = jnp.minimum(b * nt + t + 1, nb * nt - 1)
        return s // nt, s % nt

    def nfwd_map(b, t):
        b1, t1 = next_step(b, t)
        return ((b1 * nt + t1) * per, 0)

    def nbwd_map(b, t):
        b1, t1 = next_step(b, t)
        return ((b1 * nt + nt - 1 - t1) * per + per - 1, 0)

    nfwd = pl.BlockSpec((ahead, HG_WIDTH), nfwd_map)
    nbwd = pl.BlockSpec((ahead, HG_WIDTH), nbwd_map)
    out = jax.ShapeDtypeStruct((n, HG_WIDTH), F32)
    return pl.pallas_call(
        functools.partial(_mixer_kernel, seq=seq),
        grid=(nb, nt),
        in_specs=[fwd] * 5 + [bwd] * 5 + [nfwd] * 4 + [nbwd] * 4 + [col, col, col, bias]
        + [_const_spec(run.shape), _const_spec(msk.shape), _const_spec(sel.shape)],
        out_specs=[fwd, bwd, col],
        out_shape=[out, out, jax.ShapeDtypeStruct((n, NA_WIDTH), BF16)],
        scratch_shapes=[pltpu.VMEM((2, HG_HEADS, HG_DK, LANES), F32)]
        + [pltpu.VMEM((HG_GROUPS, run.shape[1], 2 * LANES), F32)] * 2
        + [pltpu.VMEM((HG_CHAINS, len(HG_LEVELS) + 1, HG_C, LANES), BF16)] * 2
        + [pltpu.VMEM((HG_CHAINS, len(HG_LEVELS) + 2, HG_DK, HG_C), BF16)] * 2
        + [pltpu.VMEM((HG_CHAINS, SUBLANES, LANES), F32)] * 2
        + [pltpu.VMEM((NA_ROWS_PER_ITER, 2 * GRID_W, WIN_ROWS * GRID_W), F32)],
        compiler_params=_params("arbitrary", "arbitrary"),
        name="mixer",
    )(p["q"], p["kf"], p["gfh"], p["gfl"], p["v"], p["q"], p["kb"], p["gbh"], p["gbl"], p["v"],
      p["q"], p["kf"], p["gfh"], p["gfl"], p["q"], p["kb"], p["gbh"], p["gbl"],
      p["nq"], p["nk"], p["nv"], w["bias"], run, msk, sel)


def _na_stage_fns(q_ref, k_ref, v_ref, bias_ref, o_ref, s_ref, seq):
    rows = seq // GRID_W
    win = WIN_ROWS * GRID_W
    assert rows & (rows - 1) == 0
    row_base = jnp.bitwise_and(pl.program_id(1) * NA_STEP_ROWS, rows - 1)
    lane = lax.broadcasted_iota(jnp.int32, (GRID_W, LANES), 1)
    first = lane < NA_DH

    def geometry(j):
        rs = [row_base + j * NA_ROWS_PER_ITER + u for u in range(NA_ROWS_PER_ITER)]
        r0s = [jnp.clip(r - WIN_ROWS // 2, 0, rows - WIN_ROWS) for r in rs]
        qrows = [pl.ds(pl.multiple_of(r * GRID_W, GRID_W), GRID_W) for r in rs]
        wrows = [pl.ds(pl.multiple_of(r0 * GRID_W, GRID_W), win) for r0 in r0s]
        return rs, r0s, qrows, wrows

    def scores(j):
        rs, r0s, qrows, wrows = geometry(j)
        for u, (r, r0, qr_, wr_) in enumerate(zip(rs, r0s, qrows, wrows)):
            qr = q_ref[qr_, :]
            zero = jnp.zeros_like(qr)
            q2 = jnp.concatenate([jnp.where(first, qr, zero), jnp.where(first, zero, qr)], axis=0)
            rho0 = WIN_ROWS - 1 - (r - r0)
            bias = jnp.concatenate(
                [jnp.concatenate([bias_ref[0, a, rho0 + 2 * m] for m in range(WIN_ROWS // 2)], axis=1)
                 for a in range(2)], axis=0)
            s_ref[u] = _dot_nt(q2, k_ref[wr_, :]) + bias

    def softmax():
        ps, ls = [], []
        for u in range(NA_ROWS_PER_ITER):
            s = s_ref[u]
            p = jnp.exp2(s - jnp.max(s, axis=-1, keepdims=True))
            ls.append(jnp.sum(p, axis=-1, keepdims=True))
            ps.append(p.astype(BF16))
        return ps, ls

    def values(j, probs):
        _, _, qrows, wrows = geometry(j)
        for p, l, qr_, wr_ in zip(*probs, qrows, wrows):
            o2 = _dot(p, v_ref[wr_, :]) * (1.0 / l)
            o_ref[qr_, :] = jnp.where(first, o2[:GRID_W], o2[GRID_W:]).astype(BF16)

    return scores, softmax, values


def _na_bias_tables(na_rpb):
    qcol = np.arange(GRID_W)
    c0 = np.clip(qcol - WIN_COLS // 2, 0, GRID_W - WIN_COLS)
    col_mask = (qcol[None, :] >= c0[:, None]) & (qcol[None, :] < c0[:, None] + WIN_COLS)
    rel_c = np.clip(qcol[None, :] - qcol[:, None], -(WIN_COLS - 1), WIN_COLS - 1) + WIN_COLS - 1
    onehot = (rel_c[:, :, None] == np.arange(2 * WIN_COLS - 1)).astype(np.float32)
    t = jnp.einsum("qkc,lhrc->lhrqk", onehot, na_rpb.astype(F32), precision=lax.Precision.HIGHEST)
    t = jnp.where(col_mask, t * LOG2E, NEG_BIG)
    pairs = jnp.concatenate([t[:, :, :-1], t[:, :, 1:]], axis=-1)
    return pairs.reshape(DEPTH, NA_HEADS // 2, 2, 2 * WIN_ROWS - 2, GRID_W, LANES)


def _lower_bounds(p):
    sm = jax.nn.softmax(p.astype(F32), axis=0)
    return jnp.cumsum(sm, axis=0) - sm[0]


def _trunk(x, b0, w, consts):
    nb, seq, _ = x.shape
    x = x.reshape(nb * seq, D_MODEL)
    for l in range(DEPTH):
        x = _ffn(x, w, l, 0, seq=seq, b0=b0)
        p = _in_proj(x, w, l, seq=seq, b0=b0)
        of, ob, ona = _mixer(p, w, l, consts, seq=seq)
        x = _ffn(x, w, l, 1, seq=seq, b0=b0, mix=(of, ob, p["sg"], ona))
    return x.reshape(nb, seq, D_MODEL)


def kernel(x_prompt, x_sample, c_prompt, c_sample, w_mod, b_mod, norm_g, ffn_w_gate, ffn_w_up, ffn_w_down,
           w_in, w_out, hg_lb_fwd, hg_lb_bwd, hg_norm_g, na_q_norm_g, na_k_norm_g, na_rpb):
    nbp = x_prompt.shape[0]
    c_all = jnp.concatenate([c_prompt, c_sample], axis=0)
    per_head = lambda g, scale: jnp.tile(g.astype(F32) * scale, (1, NA_HEADS)).reshape(DEPTH, 1, NA_WIDTH)
    w = {
        "mod": _modulation(c_all, w_mod, b_mod).reshape(DEPTH, c_all.shape[0], 9, D_MODEL),
        "norm_g": norm_g.reshape(DEPTH, 3, 1, D_MODEL),
        "wg": ffn_w_gate.astype(BF16), "wu": ffn_w_up.astype(BF16), "wd": ffn_w_down.astype(BF16),
        "w_in": w_in.astype(BF16), "w_out": w_out.astype(BF16),
        "lb_f": _lower_bounds(hg_lb_fwd).reshape(DEPTH, 1, HG_WIDTH),
        "lb_b": _lower_bounds(hg_lb_bwd).reshape(DEPTH, 1, HG_WIDTH),
        "hg_norm_g": hg_norm_g.reshape(DEPTH, 1, LANES),
        "qg": per_head(na_q_norm_g, NA_DH ** -0.5 * LOG2E), "kg": per_head(na_k_norm_g, 1.0),
        "bias": _na_bias_tables(na_rpb),
    }
    consts = _hgrn_constants()
    y_prompt = _trunk(x_prompt, 0, w, consts)
    y_sample = _trunk(x_sample, nbp, w, consts)
    return (y_prompt, y_sample)
```

```python
import functools
import math

import numpy as np
import jax
import jax.numpy as jnp
from jax import lax
from jax.experimental import pallas as pl
from jax.experimental.pallas import tpu as pltpu

F32 = jnp.float32
BF16 = jnp.bfloat16

D_MODEL = 1024
DEPTH = 4
GRID_W = 64
HG_HEADS = 4
HG_DK = 128
HG_WIDTH = 512
NA_HEADS = 8
NA_DH = 64
NA_WIDTH = 512
WIN_ROWS = 8
WIN_COLS = 16
D_FF = 2816
IN_WIDTH = 4096
EPS = 1e-6
LOG2E = math.log2(math.e)

LANES = 128
SUBLANES = 8
TM = 1024
TM_MIX = 512
TM_PROJ = 1024
HG_C = 128
HG_R = 512
HG_AHEAD = 2
HG_LEVELS = tuple(HG_C >> i for i in range(HG_C.bit_length() - 1))
HG_COARSE = tuple(s for s in HG_LEVELS if s // 2 >= SUBLANES)
HG_FINE = tuple(s for s in HG_LEVELS if s // 2 < SUBLANES)
HG_GROUPS = 2 * (HG_HEADS // 2)
HG_CHAINS = 2 * HG_HEADS
NA_STEP_ROWS = (NA_HEADS // 2) * HG_R // GRID_W
NA_ROWS_PER_ITER = NA_STEP_ROWS // (HG_R // HG_C)
NEG_BIG = -1e30
VMEM_LIMIT = 56 * 1024 * 1024

NT_DIMS = (((1,), (1,)), ((), ()))
TN_DIMS = (((0,), (0,)), ((), ()))


def _dot(a, b):
    return jnp.dot(a, b, preferred_element_type=F32)


def _dot_nt(a, b):
    return lax.dot_general(a, b, NT_DIMS, preferred_element_type=F32)


def _sigmoid(x):
    return 1.0 / (1.0 + jnp.exp(-x))


def _silu(x):
    return x * _sigmoid(x)


def _layer_spec(shape, *lead):
    tail = (0,) * len(shape)
    return pl.BlockSpec((None,) * len(lead) + tuple(shape), lambda *_: tuple(lead) + tail,
                        pipeline_mode=pl.Buffered(1))


def _const_spec(shape):
    return _layer_spec(shape)


def _params(*sem):
    return pltpu.CompilerParams(dimension_semantics=sem, vmem_limit_bytes=VMEM_LIMIT)


def _mod_kernel(c_ref, w_ref, b_ref, o_ref):
    a = _silu(c_ref[...]).astype(BF16)
    o_ref[0] = _dot(a, w_ref[0].astype(BF16)) + b_ref[0]


def _modulation(c_all, w_mod, b_mod):
    nb = c_all.shape[0]
    tn = 1152
    return pl.pallas_call(
        _mod_kernel,
        grid=(DEPTH, 9 * D_MODEL // tn),
        in_specs=[
            pl.BlockSpec((nb, D_MODEL), lambda l, j: (0, 0)),
            pl.BlockSpec((1, D_MODEL, tn), lambda l, j: (l, 0, j)),
            pl.BlockSpec((1, 1, tn), lambda l, j: (l, 0, j)),
        ],
        out_specs=pl.BlockSpec((1, nb, tn), lambda l, j: (l, 0, j)),
        out_shape=jax.ShapeDtypeStruct((DEPTH, nb, 9 * D_MODEL), F32),
        compiler_params=_params("arbitrary", "arbitrary"),
        name="modulation",
    )(c_all, w_mod, b_mod.reshape(DEPTH, 1, 9 * D_MODEL))


def _norm_modulate(x, g, shift, scale):
    ms = jnp.mean(x * x, axis=-1, keepdims=True)
    h = (x * lax.rsqrt(ms + EPS)) * g
    return h * (1.0 + scale) + shift


def _mod_spec(l, b0, seq, tm):
    return pl.BlockSpec((None, 1, 9, D_MODEL), lambda i: (l, b0 + i * tm // seq, 0, 0))


def _ffn_kernel(*refs, sub, with_mix):
    if with_mix:
        (x_ref, of_ref, ob_ref, sg_ref, ona_ref, hng_ref, wout_ref,
         mod_ref, ng_ref, wg_ref, wu_ref, wd_ref, o_ref) = refs
    else:
        x_ref, mod_ref, ng_ref, wg_ref, wu_ref, wd_ref, o_ref = refs
    x = x_ref[...]
    if with_mix:
        heads = []
        for h in range(HG_HEADS):
            cols = slice(h * LANES, (h + 1) * LANES)
            o = of_ref[:, cols] + ob_ref[:, cols]
            ms = jnp.mean(o * o, axis=-1, keepdims=True)
            o = (o * lax.rsqrt(ms + EPS)) * hng_ref[...]
            heads.append((o * sg_ref[:, cols].astype(F32)).astype(BF16))
        ohg = jnp.concatenate(heads, axis=-1)
        mix = _dot(ohg, wout_ref[0:HG_WIDTH, :]) + _dot(ona_ref[...], wout_ref[HG_WIDTH:, :])
        x = x + mod_ref[0, 5:6, :] * mix
    shift = mod_ref[0, 3 * sub:3 * sub + 1, :]
    scale = mod_ref[0, 3 * sub + 1:3 * sub + 2, :]
    gate = mod_ref[0, 3 * sub + 2:3 * sub + 3, :]
    h = _norm_modulate(x, ng_ref[...], shift, scale).astype(BF16)
    y = None
    for lo, hi in ((0, 1024), (1024, 2048), (2048, D_FF)):
        g = _dot(h, wg_ref[:, lo:hi])
        u = _dot(h, wu_ref[:, lo:hi])
        a = (_silu(g) * u).astype(BF16)
        ys = _dot(a, wd_ref[lo:hi, :])
        y = ys if y is None else y + ys
    o_ref[...] = x + 0.5 * gate * y


def _ffn(x, w, l, ffn_idx, *, seq, b0, mix=None):
    n = x.shape[0]
    sub = 2 * ffn_idx
    tm = TM if mix is None else TM_MIX
    tok = lambda width: pl.BlockSpec((tm, width), lambda i: (i, 0))
    args, specs = [x], [tok(D_MODEL)]
    if mix is not None:
        of, ob, sg, ona = mix
        args += [of, ob, sg, ona, w["hg_norm_g"], w["w_out"]]
        specs += [tok(HG_WIDTH), tok(HG_WIDTH), tok(HG_WIDTH), tok(NA_WIDTH),
                  _layer_spec((1, LANES), l), _layer_spec((D_MODEL, D_MODEL), l)]
    args += [w["mod"], w["norm_g"], w["wg"], w["wu"], w["wd"]]
    specs += [_mod_spec(l, b0, seq, tm), _layer_spec((1, D_MODEL), l, sub),
              _layer_spec((D_MODEL, D_FF), l, ffn_idx), _layer_spec((D_MODEL, D_FF), l, ffn_idx),
              _layer_spec((D_FF, D_MODEL), l, ffn_idx)]
    return pl.pallas_call(
        functools.partial(_ffn_kernel, sub=sub, with_mix=mix is not None),
        grid=(n // tm,),
        in_specs=specs,
        out_specs=tok(D_MODEL),
        out_shape=jax.ShapeDtypeStruct((n, D_MODEL), F32),
        compiler_params=_params("arbitrary"),
        name="ffn_mix" if mix is not None else "ffn",
    )(*args)


PROJ_OUTS = ("q", "kf", "kb", "gfh", "gfl", "gbh", "gbl", "v", "sg", "nq", "nk", "nv")


def _proj_kernel(x_ref, mod_ref, ng_ref, w_ref, lbf_ref, lbb_ref, qg_ref, kg_ref, *outs):
    o = dict(zip(PROJ_OUTS, outs))
    h = _norm_modulate(x_ref[...], ng_ref[...], mod_ref[0, 3:4, :], mod_ref[0, 4:5, :]).astype(BF16)
    seg = lambda i: _dot(h, w_ref[:, i * HG_WIDTH:(i + 1) * HG_WIDTH])

    def decay(i, lb_ref, k_ref, hi_ref, lo_ref):
        lb = lb_ref[...]
        f = lb + (1.0 - lb) * _sigmoid(seg(i))
        k_ref[...] = (1.0 - f).astype(BF16)
        g = jnp.log2(f)
        hi = g.astype(BF16)
        hi_ref[...] = hi
        lo_ref[...] = (g - hi.astype(F32)).astype(BF16)

    lane = lax.broadcasted_iota(jnp.int32, (x_ref.shape[0], LANES), 1)
    first = lane < NA_DH

    def head_norm(i, g_ref, o_ref):
        x = seg(i)
        for t in range(NA_WIDTH // LANES):
            cols = slice(t * LANES, (t + 1) * LANES)
            xt = x[:, cols]
            x2 = xt * xt
            sa = jnp.sum(jnp.where(first, x2, 0.0), axis=-1, keepdims=True)
            sb = jnp.sum(jnp.where(first, 0.0, x2), axis=-1, keepdims=True)
            ms = jnp.where(first, sa, sb) * (1.0 / NA_DH)
            o_ref[:, cols] = ((xt * lax.rsqrt(ms + EPS)) * g_ref[:, cols]).astype(BF16)

    decay(1, lbf_ref, o["kf"], o["gfh"], o["gfl"])
    decay(2, lbb_ref, o["kb"], o["gbh"], o["gbl"])
    head_norm(5, qg_ref, o["nq"])
    head_norm(6, kg_ref, o["nk"])
    o["q"][...] = _silu(seg(0)).astype(BF16)
    o["sg"][...] = _silu(seg(4)).astype(BF16)
    o["v"][...] = seg(3).astype(BF16)
    o["nv"][...] = seg(7).astype(BF16)


def _in_proj(x, w, l, *, seq, b0):
    n = x.shape[0]
    vec = _layer_spec((1, HG_WIDTH), l)
    outs = pl.pallas_call(
        _proj_kernel,
        grid=(n // TM_PROJ,),
        in_specs=[
            pl.BlockSpec((TM_PROJ, D_MODEL), lambda i: (i, 0)),
            _mod_spec(l, b0, seq, TM_PROJ),
            _layer_spec((1, D_MODEL), l, 1),
            _layer_spec((D_MODEL, IN_WIDTH), l),
            vec, vec, vec, vec,
        ],
        out_specs=[pl.BlockSpec((TM_PROJ, HG_WIDTH), lambda i: (i, 0))] * len(PROJ_OUTS),
        out_shape=[jax.ShapeDtypeStruct((n, HG_WIDTH), BF16)] * len(PROJ_OUTS),
        compiler_params=_params("arbitrary"),
        name="in_proj",
    )(x, w["mod"], w["norm_g"], w["w_in"], w["lb_f"], w["lb_b"], w["qg"], w["kg"])
    return dict(zip(PROJ_OUTS, outs))


def _hgrn_constants():
    c = HG_C
    idx = np.arange(c)
    t = idx[None, :]
    i = idx[:, None]
    run_blocks = [t <= i]
    sels, masks = [], []
    for s in HG_LEVELS:
        half = s // 2
        p = idx % s
        start = idx - p
        m = (start + half - 1)[:, None]
        is_q = p >= half
        masks.append(is_q[:, None] & ~is_q[None, :] & (start[:, None] == start[None, :]))
        if s in HG_FINE:
            qrow = (t >= m + 1) & (t <= i)
            krow = (t >= i + 1) & (t <= m)
            run_blocks.append(np.where(is_q[:, None], qrow, krow))
            sels.append(np.broadcast_to(is_q[:, None], (c, LANES)))
    masks.append(np.eye(c, dtype=bool))
    assert len(masks) % 2 == 0
    both = lambda a: np.stack([a, a[:, ::-1, ::-1]])
    run = both(np.stack(run_blocks).astype(np.float32))
    run = np.concatenate([run, run], axis=-1).reshape(2, -1, 2 * c)
    msk = both(np.stack(masks).astype(np.float32))
    msk = msk.reshape(2, len(masks) // 2, 2, c, c).transpose(0, 1, 3, 2, 4).reshape(2, -1, c, 2 * c)
    sel = np.stack(sels).astype(np.float32)
    sel = np.stack([sel, sel[:, ::-1, :]])
    return jnp.asarray(run, dtype=BF16), jnp.asarray(msk, dtype=BF16), jnp.asarray(sel, dtype=F32)


def _hgrn_operands(d, b, a_fine, q, kk, kkb, selq, x_ref, xt_ref, e_ref):
    c = HG_C
    nlev = len(HG_LEVELS)

    def put(l, x):
        xb = x.astype(BF16)
        x_ref[l] = xb
        xt_ref[l] = xb.T

    for l, s in enumerate(HG_COARSE):
        half = s // 2
        parts = []
        for blk in range(c // half):
            rows = slice(blk * half, (blk + 1) * half)
            base = (blk // 2) * s
            if d == 0:
                m, is_q = base + half - 1, blk % 2 == 1
            else:
                m, is_q = base + half, blk % 2 == 0
            bm = b[m:m + 1]
            if is_q:
                parts.append(q[rows] * jnp.exp2(b[rows] - bm))
            else:
                parts.append(kk[rows] * jnp.exp2(bm - b[rows]))
        put(l, jnp.concatenate(parts, axis=0))
    for lf in range(len(HG_FINE)):
        e = jnp.exp2(a_fine[lf * c:(lf + 1) * c])
        put(len(HG_COARSE) + lf, jnp.where(selq[lf], q, kk) * e)
    last = c - 1 if d == 0 else 0
    b_tot = b[last:last + 1]
    x_ref[nlev] = (q * jnp.exp2(b)).astype(BF16)
    xt_ref[nlev] = kkb.T
    xt_ref[nlev + 1] = (kk * jnp.exp2(b_tot - b)).astype(BF16).T
    e_ref[...] = jnp.broadcast_to(jnp.exp2(b_tot), e_ref.shape)


def _mixer_kernel(qf_ref, kf_ref, gfh_ref, gfl_ref, vf_ref, qb_ref, kb_ref, gbh_ref, gbl_ref, vb_ref,
                  nqf_ref, nkf_ref, ngfh_ref, ngfl_ref, nqb_ref, nkb_ref, ngbh_ref, ngbl_ref,
                  aq_ref, ak_ref, av_ref, bias_ref,
                  run_ref, msk_ref, sel_ref, of_ref, ob_ref, ona_ref, st_ref,
                  a0_ref, a1_ref, x0_ref, x1_ref, xt0_ref, xt1_ref, e0_ref, e1_ref, ns_ref, *, seq):
    c = HG_C
    nchunk = HG_R // c
    assert nchunk % 2 == 0 and nchunk >= 2 * HG_AHEAD
    na = _na_stage_fns(aq_ref, ak_ref, av_ref, bias_ref, ona_ref, ns_ref, seq)
    nlev = len(HG_LEVELS)
    npair = (nlev + 1) // 2
    a_refs, x_refs, e_refs = (a0_ref, a1_ref), (x0_ref, x1_ref), (e0_ref, e1_ref)
    xt_refs = (xt0_ref, xt1_ref)

    cur = ((qf_ref, kf_ref, gfh_ref, gfl_ref), (qb_ref, kb_ref, gbh_ref, gbl_ref))
    nxt = ((nqf_ref, nkf_ref, ngfh_ref, ngfl_ref), (nqb_ref, nkb_ref, ngbh_ref, ngbl_ref))
    vo = ((vf_ref, of_ref), (vb_ref, ob_ref))
    zero = jnp.zeros((c, LANES), BF16)

    def chunk_rows(j, d):
        jj = j if d == 0 else nchunk - 1 - j
        return pl.ds(pl.multiple_of(jj * c, c), c)

    def cur_src(j):
        return lambda d: (cur[d], chunk_rows(j, d))

    def nxt_src(jn):
        return lambda d: (nxt[d], pl.ds((jn if d == 0 else HG_AHEAD - 1 - jn) * c, c))

    def run_sums(src, slot):
        for d in range(2):
            (_, _, gh_ref, gl_ref), rows = src(d)
            for hp in range(HG_HEADS // 2):
                cols2 = slice(hp * 2 * LANES, (hp + 1) * 2 * LANES)
                g2 = jnp.concatenate([gh_ref[rows, cols2], gl_ref[rows, cols2]], axis=0)
                a_refs[slot][2 * d + hp] = _dot(run_ref[d], g2)

    def operands(src, slot, d, h, selq):
        (q_ref, k_ref, _, _), rows = src(d)
        cols = slice(h * LANES, (h + 1) * LANES)
        lanes = slice((h % 2) * LANES, (h % 2 + 1) * LANES)
        a_ref = a_refs[slot]
        ci = d * HG_HEADS + h
        kkb = k_ref[rows, cols]
        _hgrn_operands(d, a_ref[2 * d + h // 2, 0:c, lanes], a_ref[2 * d + h // 2, c:, lanes],
                       q_ref[rows, cols].astype(F32), kkb.astype(F32), kkb, selq,
                       x_refs[slot].at[ci], xt_refs[slot].at[ci], e_refs[slot].at[ci])

    def level_pair(j, slot, d, h, i):
        q_ref = cur[d][0]
        rows = chunk_rows(j, d)
        cols = slice(h * LANES, (h + 1) * LANES)
        ci = d * HG_HEADS + h
        x_ref, xt_ref = x_refs[slot].at[ci], xt_refs[slot].at[ci]
        l2 = x_ref[2 * i + 1] if 2 * i + 1 < nlev else q_ref[rows, cols]
        g2 = _dot(jnp.concatenate([x_ref[2 * i], l2], axis=1),
                  jnp.concatenate([jnp.concatenate([xt_ref[2 * i], zero], axis=1),
                                   jnp.concatenate([zero, xt_ref[2 * i + 1]], axis=1)], axis=0))
        g2 = g2.astype(BF16) * msk_ref[d, i]
        return g2[:, :c] + g2[:, c:]

    def finish(j, slot, d, h, p):
        v_ref, o_ref = vo[d]
        rows = chunk_rows(j, d)
        cols = slice(h * LANES, (h + 1) * LANES)
        ci = d * HG_HEADS + h
        vb = v_ref[rows, cols]
        st = st_ref[d, h]
        o_ref[rows, cols] = _dot(jnp.concatenate([p, x_refs[slot][ci, nlev]], axis=1),
                                 jnp.concatenate([vb, st.astype(BF16)], axis=0))
        e_col = jnp.transpose(jnp.broadcast_to(e_refs[slot][ci, 0:1, :], (c, LANES)))
        st_ref[d, h] = st * e_col + _dot(xt_refs[slot][ci, nlev + 1], vb)

    def sel_masks():
        return [[sel_ref[d, lf] != 0.0 for lf in range(len(HG_FINE))] for d in range(2)]

    chains = [(d, h) for d in range(2) for h in range(HG_HEADS)]
    assert len(chains) % npair == 0

    def step(j, slot, ahead2, ahead1):
        run_sums(ahead2, slot)
        na_scores, na_softmax, na_values = na
        na_scores(j)
        selq = sel_masks()
        ps = [None] * len(chains)
        per = len(chains) // npair
        probs = None
        for i in range(npair):
            for ci, (d, h) in enumerate(chains):
                g = level_pair(j, slot, d, h, i)
                ps[ci] = g if ps[ci] is None else ps[ci] + g
            for d, h in chains[i * per:(i + 1) * per]:
                operands(ahead1, 1 - slot, d, h, selq[d])
            if i == npair // 2 - 1:
                probs = na_softmax()
        na_values(j, probs)
        for ci, (d, h) in enumerate(chains):
            finish(j, slot, d, h, ps[ci])

    def body(i, carry):
        step(2 * i, 0, cur_src(2 * i + 2), cur_src(2 * i + 1))
        step(2 * i + 1, 1, cur_src(2 * i + 3), cur_src(2 * i + 2))
        return carry

    @pl.when(pl.program_id(1) == 0)
    def _():
        st_ref[...] = jnp.zeros_like(st_ref)

    @pl.when(jnp.logical_and(pl.program_id(0) == 0, pl.program_id(1) == 0))
    def _():
        run_sums(cur_src(0), 0)
        run_sums(cur_src(1), 1)
        selq = sel_masks()
        for d, h in chains:
            operands(cur_src(0), 0, d, h, selq[d])

    lax.fori_loop(0, nchunk // 2 - 1, body, 0)
    step(nchunk - 2, 0, nxt_src(0), cur_src(nchunk - 1))
    step(nchunk - 1, 1, nxt_src(1), nxt_src(0))


def _mixer(p, w, l, consts, *, seq):
    n = p["q"].shape[0]
    nb, nt = n // seq, seq // HG_R
    rows = seq // GRID_W
    assert nt * NA_STEP_ROWS == (NA_HEADS // 2) * rows and rows % NA_STEP_ROWS == 0
    run, msk, sel = consts
    head_pair = lambda t: (t * NA_STEP_ROWS) // rows
    col = pl.BlockSpec((seq, LANES), lambda b, t: (b, head_pair(t)))
    bias = pl.BlockSpec((None, 1, 2, 2 * WIN_ROWS - 2, GRID_W, LANES),
                        lambda b, t: (l, head_pair(t), 0, 0, 0, 0))
    ahead = HG_AHEAD * HG_C
    per = HG_R // ahead
    fwd = pl.BlockSpec((HG_R, HG_WIDTH), lambda b, t: (b * nt + t, 0))
    bwd = pl.BlockSpec((HG_R, HG_WIDTH), lambda b, t: (b * nt + nt - 1 - t, 0))
    def next_step(b, t):
        s = jnp.minimum(b * nt + t + 1, nb * nt - 1)
        return s // nt, s % nt

    def nfwd_map(b, t):
        b1, t1 = next_step(b, t)
        return ((b1 * nt + t1) * per, 0)

    def nbwd_map(b, t):
        b1, t1 = next_step(b, t)
        return ((b1 * nt + nt - 1 - t1) * per + per - 1, 0)

    nfwd = pl.BlockSpec((ahead, HG_WIDTH), nfwd_map)
    nbwd = pl.BlockSpec((ahead, HG_WIDTH), nbwd_map)
    out = jax.ShapeDtypeStruct((n, HG_WIDTH), F32)
    return pl.pallas_call(
        functools.partial(_mixer_kernel, seq=seq),
        grid=(nb, nt),
        in_specs=[fwd] * 5 + [bwd] * 5 + [nfwd] * 4 + [nbwd] * 4 + [col, col, col, bias]
        + [_const_spec(run.shape), _const_spec(msk.shape), _const_spec(sel.shape)],
        out_specs=[fwd, bwd, col],
        out_shape=[out, out, jax.ShapeDtypeStruct((n, NA_WIDTH), BF16)],
        scratch_shapes=[pltpu.VMEM((2, HG_HEADS, HG_DK, LANES), F32)]
        + [pltpu.VMEM((HG_GROUPS, run.shape[1], 2 * LANES), F32)] * 2
        + [pltpu.VMEM((HG_CHAINS, len(HG_LEVELS) + 1, HG_C, LANES), BF16)] * 2
        + [pltpu.VMEM((HG_CHAINS, len(HG_LEVELS) + 2, HG_DK, HG_C), BF16)] * 2
        + [pltpu.VMEM((HG_CHAINS, SUBLANES, LANES), F32)] * 2
        + [pltpu.VMEM((NA_ROWS_PER_ITER, 2 * GRID_W, WIN_ROWS * GRID_W), F32)],
        compiler_params=_params("arbitrary", "arbitrary"),
        name="mixer",
    )(p["q"], p["kf"], p["gfh"], p["gfl"], p["v"], p["q"], p["kb"], p["gbh"], p["gbl"], p["v"],
      p["q"], p["kf"], p["gfh"], p["gfl"], p["q"], p["kb"], p["gbh"], p["gbl"],
      p["nq"], p["nk"], p["nv"], w["bias"], run, msk, sel)


def _na_stage_fns(q_ref, k_ref, v_ref, bias_ref, o_ref, s_ref, seq):
    rows = seq // GRID_W
    win = WIN_ROWS * GRID_W
    assert rows & (rows - 1) == 0
    row_base = jnp.bitwise_and(pl.program_id(1) * NA_STEP_ROWS, rows - 1)
    lane = lax.broadcasted_iota(jnp.int32, (GRID_W, LANES), 1)
    first = lane < NA_DH

    def geometry(j):
        rs = [row_base + j * NA_ROWS_PER_ITER + u for u in range(NA_ROWS_PER_ITER)]
        r0s = [jnp.clip(r - WIN_ROWS // 2, 0, rows - WIN_ROWS) for r in rs]
        qrows = [pl.ds(pl.multiple_of(r * GRID_W, GRID_W), GRID_W) for r in rs]
        wrows = [pl.ds(pl.multiple_of(r0 * GRID_W, GRID_W), win) for r0 in r0s]
        return rs, r0s, qrows, wrows

    def scores(j):
        rs, r0s, qrows, wrows = geometry(j)
        for u, (r, r0, qr_, wr_) in enumerate(zip(rs, r0s, qrows, wrows)):
            qr = q_ref[qr_, :]
            zero = jnp.zeros_like(qr)
            q2 = jnp.concatenate([jnp.where(first, qr, zero), jnp.where(first, zero, qr)], axis=0)
            rho0 = WIN_ROWS - 1 - (r - r0)
            bias = jnp.concatenate(
                [jnp.concatenate([bias_ref[0, a, rho0 + 2 * m] for m in range(WIN_ROWS // 2)], axis=1)
                 for a in range(2)], axis=0)
            s_ref[u] = _dot_nt(q2, k_ref[wr_, :]) + bias

    def softmax():
        ps, ls = [], []
        for u in range(NA_ROWS_PER_ITER):
            s = s_ref[u]
            p = jnp.exp2(s - jnp.max(s, axis=-1, keepdims=True))
            ls.append(jnp.sum(p, axis=-1, keepdims=True))
            ps.append(p.astype(BF16))
        return ps, ls

    def values(j, probs):
        _, _, qrows, wrows = geometry(j)
        for p, l, qr_, wr_ in zip(*probs, qrows, wrows):
            o2 = _dot(p, v_ref[wr_, :]) * (1.0 / l)
            o_ref[qr_, :] = jnp.where(first, o2[:GRID_W], o2[GRID_W:]).astype(BF16)

    return scores, softmax, values


def _na_bias_tables(na_rpb):
    qcol = np.arange(GRID_W)
    c0 = np.clip(qcol - WIN_COLS // 2, 0, GRID_W - WIN_COLS)
    col_mask = (qcol[None, :] >= c0[:, None]) & (qcol[None, :] < c0[:, None] + WIN_COLS)
    rel_c = np.clip(qcol[None, :] - qcol[:, None], -(WIN_COLS - 1), WIN_COLS - 1) + WIN_COLS - 1
    onehot = (rel_c[:, :, None] == np.arange(2 * WIN_COLS - 1)).astype(np.float32)
    t = jnp.einsum("qkc,lhrc->lhrqk", onehot, na_rpb.astype(F32), precision=lax.Precision.HIGHEST)
    t = jnp.where(col_mask, t * LOG2E, NEG_BIG)
    pairs = jnp.concatenate([t[:, :, :-1], t[:, :, 1:]], axis=-1)
    return pairs.reshape(DEPTH, NA_HEADS // 2, 2, 2 * WIN_ROWS - 2, GRID_W, LANES)


def _lower_bounds(p):
    sm = jax.nn.softmax(p.astype(F32), axis=0)
    return jnp.cumsum(sm, axis=0) - sm[0]


def _trunk(x, b0, w, consts):
    nb, seq, _ = x.shape
    x = x.reshape(nb * seq, D_MODEL)
    for l in range(DEPTH):
        x = _ffn(x, w, l, 0, seq=seq, b0=b0)
        p = _in_proj(x, w, l, seq=seq, b0=b0)
        of, ob, ona = _mixer(p, w, l, consts, seq=seq)
        x = _ffn(x, w, l, 1, seq=seq, b0=b0, mix=(of, ob, p["sg"], ona))
    return x.reshape(nb, seq, D_MODEL)


def kernel(x_prompt, x_sample, c_prompt, c_sample, w_mod, b_mod, norm_g, ffn_w_gate, ffn_w_up, ffn_w_down,
           w_in, w_out, hg_lb_fwd, hg_lb_bwd, hg_norm_g, na_q_norm_g, na_k_norm_g, na_rpb):
    nbp = x_prompt.shape[0]
    c_all = jnp.concatenate([c_prompt, c_sample], axis=0)
    per_head = lambda g, scale: jnp.tile(g.astype(F32) * scale, (1, NA_HEADS)).reshape(DEPTH, 1, NA_WIDTH)
    w = {
        "mod": _modulation(c_all, w_mod, b_mod).reshape(DEPTH, c_all.shape[0], 9, D_MODEL),
        "norm_g": norm_g.reshape(DEPTH, 3, 1, D_MODEL),
        "wg": ffn_w_gate.astype(BF16), "wu": ffn_w_up.astype(BF16), "wd": ffn_w_down.astype(BF16),
        "w_in": w_in.astype(BF16), "w_out": w_out.astype(BF16),
        "lb_f": _lower_bounds(hg_lb_fwd).reshape(DEPTH, 1, HG_WIDTH),
        "lb_b": _lower_bounds(hg_lb_bwd).reshape(DEPTH, 1, HG_WIDTH),
        "hg_norm_g": hg_norm_g.reshape(DEPTH, 1, LANES),
        "qg": per_head(na_q_norm_g, NA_DH ** -0.5 * LOG2E), "kg": per_head(na_k_norm_g, 1.0),
        "bias": _na_bias_tables(na_rpb),
    }
    consts = _hgrn_constants()
    y_prompt = _trunk(x_prompt, 0, w, consts)
    y_sample = _trunk(x_sample, nbp, w, consts)
    return (y_prompt, y_sample)
```

```python
import functools
import math

import numpy as np
import jax
import jax.numpy as jnp
from jax import lax
from jax.experimental import pallas as pl
from jax.experimental.pallas import tpu as pltpu

F32 = jnp.float32
BF16 = jnp.bfloat16

D_MODEL = 1024
DEPTH = 4
GRID_W = 64
HG_HEADS = 4
HG_DK = 128
HG_WIDTH = 512
NA_HEADS = 8
NA_DH = 64
NA_WIDTH = 512
WIN_ROWS = 8
WIN_COLS = 16
D_FF = 2816
IN_WIDTH = 4096
EPS = 1e-6
LOG2E = math.log2(math.e)

LANES = 128
SUBLANES = 8
TM = 1024
TM_MIX = 1024
TM_PROJ = 1024
HG_C = 128
HG_R = 512
HG_AHEAD = 2
HG_LEVELS = tuple(HG_C >> i for i in range(HG_C.bit_length() - 1))
HG_COARSE = tuple(s for s in HG_LEVELS if s // 2 >= SUBLANES)
HG_FINE = tuple(s for s in HG_LEVELS if s // 2 < SUBLANES)
HG_GROUPS = 2 * (HG_HEADS // 2)
HG_CHAINS = 2 * HG_HEADS
NA_STEP_ROWS = (NA_HEADS // 2) * HG_R // GRID_W
NA_ROWS_PER_ITER = NA_STEP_ROWS // (HG_R // HG_C)
NEG_BIG = -1e30
VMEM_LIMIT = 58 * 1024 * 1024

NT_DIMS = (((1,), (1,)), ((), ()))
TN_DIMS = (((0,), (0,)), ((), ()))


def _dot(a, b):
    return jnp.dot(a, b, preferred_element_type=F32)


def _dot_nt(a, b):
    return lax.dot_general(a, b, NT_DIMS, preferred_element_type=F32)


def _sigmoid(x):
    return 1.0 / (1.0 + jnp.exp(-x))


def _silu(x):
    return x * _sigmoid(x)


def _layer_spec(shape, *lead):
    tail = (0,) * len(shape)
    return pl.BlockSpec((None,) * len(lead) + tuple(shape), lambda *_: tuple(lead) + tail,
                        pipeline_mode=pl.Buffered(1))


def _const_spec(shape):
    return _layer_spec(shape)


def _params(*sem):
    return pltpu.CompilerParams(dimension_semantics=sem, vmem_limit_bytes=VMEM_LIMIT)


def _mod_kernel(c_ref, w_ref, b_ref, o_ref):
    a = _silu(c_ref[...]).astype(BF16)
    o_ref[0] = _dot(a, w_ref[0].astype(BF16)) + b_ref[0]


def _modulation(c_all, w_mod, b_mod):
    nb = c_all.shape[0]
    tn = 1152
    return pl.pallas_call(
        _mod_kernel,
        grid=(DEPTH, 9 * D_MODEL // tn),
        in_specs=[
            pl.BlockSpec((nb, D_MODEL), lambda l, j: (0, 0)),
            pl.BlockSpec((1, D_MODEL, tn), lambda l, j: (l, 0, j)),
            pl.BlockSpec((1, 1, tn), lambda l, j: (l, 0, j)),
        ],
        out_specs=pl.BlockSpec((1, nb, tn), lambda l, j: (l, 0, j)),
        out_shape=jax.ShapeDtypeStruct((DEPTH, nb, 9 * D_MODEL), F32),
        compiler_params=_params("arbitrary", "arbitrary"),
        name="modulation",
    )(c_all, w_mod, b_mod.reshape(DEPTH, 1, 9 * D_MODEL))


def _norm_modulate(x, g, shift, scale):
    ms = jnp.mean(x * x, axis=-1, keepdims=True)
    h = (x * lax.rsqrt(ms + EPS)) * g
    return h * (1.0 + scale) + shift


def _mod_spec(l, b0, seq, tm):
    return pl.BlockSpec((None, 1, 9, D_MODEL), lambda i: (l, b0 + i * tm // seq, 0, 0))


def _ffn_kernel(*refs, sub, with_mix):
    if with_mix:
        (x_ref, of_ref, ob_ref, sg_ref, ona_ref, hng_ref, wout_ref,
         mod_ref, ng_ref, wg_ref, wu_ref, wd_ref, o_ref) = refs
    else:
        x_ref, mod_ref, ng_ref, wg_ref, wu_ref, wd_ref, o_ref = refs
    x = x_ref[...]
    if with_mix:
        heads = []
        for h in range(HG_HEADS):
            cols = slice(h * LANES, (h + 1) * LANES)
            o = of_ref[:, cols].astype(F32) + ob_ref[:, cols].astype(F32)
            ms = jnp.mean(o * o, axis=-1, keepdims=True)
            o = (o * lax.rsqrt(ms + EPS)) * hng_ref[...]
            heads.append((o * sg_ref[:, cols].astype(F32)).astype(BF16))
        ohg = jnp.concatenate(heads, axis=-1)
        mix = _dot(ohg, wout_ref[0:HG_WIDTH, :]) + _dot(ona_ref[...], wout_ref[HG_WIDTH:, :])
        x = x + mod_ref[0, 5:6, :] * mix
    shift = mod_ref[0, 3 * sub:3 * sub + 1, :]
    scale = mod_ref[0, 3 * sub + 1:3 * sub + 2, :]
    gate = mod_ref[0, 3 * sub + 2:3 * sub + 3, :]
    h = _norm_modulate(x, ng_ref[...], shift, scale).astype(BF16)
    y = None
    for lo, hi in ((0, 1024), (1024, 2048), (2048, D_FF)):
        g = _dot(h, wg_ref[:, lo:hi])
        u = _dot(h, wu_ref[:, lo:hi])
        a = (_silu(g) * u).astype(BF16)
        ys = _dot(a, wd_ref[lo:hi, :])
        y = ys if y is None else y + ys
    o_ref[...] = x + 0.5 * gate * y


def _ffn(x, w, l, ffn_idx, *, seq, b0, mix=None):
    n = x.shape[0]
    sub = 2 * ffn_idx
    tm = TM if mix is None else TM_MIX
    tok = lambda width: pl.BlockSpec((tm, width), lambda i: (i, 0))
    args, specs = [x], [tok(D_MODEL)]
    if mix is not None:
        of, ob, sg, ona = mix
        args += [of, ob, sg, ona, w["hg_norm_g"], w["w_out"]]
        specs += [tok(HG_WIDTH), tok(HG_WIDTH), tok(HG_WIDTH), tok(NA_WIDTH),
                  _layer_spec((1, LANES), l), _layer_spec((D_MODEL, D_MODEL), l)]
    args += [w["mod"], w["norm_g"], w["wg"], w["wu"], w["wd"]]
    specs += [_mod_spec(l, b0, seq, tm), _layer_spec((1, D_MODEL), l, sub),
              _layer_spec((D_MODEL, D_FF), l, ffn_idx), _layer_spec((D_MODEL, D_FF), l, ffn_idx),
              _layer_spec((D_FF, D_MODEL), l, ffn_idx)]
    return pl.pallas_call(
        functools.partial(_ffn_kernel, sub=sub, with_mix=mix is not None),
        grid=(n // tm,),
        in_specs=specs,
        out_specs=tok(D_MODEL),
        out_shape=jax.ShapeDtypeStruct((n, D_MODEL), F32),
        compiler_params=_params("arbitrary"),
        name="ffn_mix" if mix is not None else "ffn",
    )(*args)


PROJ_OUTS = ("q", "kf", "kb", "gfh", "gfl", "gbh", "gbl", "v", "sg", "nq", "nk", "nv")


def _proj_kernel(x_ref, mod_ref, ng_ref, w_ref, lbf_ref, lbb_ref, qg_ref, kg_ref, *outs):
    o = dict(zip(PROJ_OUTS, outs))
    h = _norm_modulate(x_ref[...], ng_ref[...], mod_ref[0, 3:4, :], mod_ref[0, 4:5, :]).astype(BF16)
    seg = lambda i: _dot(h, w_ref[:, i * HG_WIDTH:(i + 1) * HG_WIDTH])

    def decay(i, lb_ref, k_ref, hi_ref, lo_ref):
        lb = lb_ref[...]
        f = lb + (1.0 - lb) * _sigmoid(seg(i))
        k_ref[...] = (1.0 - f).astype(BF16)
        g = jnp.log2(f)
        hi = g.astype(BF16)
        hi_ref[...] = hi
        lo_ref[...] = (g - hi.astype(F32)).astype(BF16)

    lane = lax.broadcasted_iota(jnp.int32, (x_ref.shape[0], LANES), 1)
    first = lane < NA_DH

    def head_norm(i, g_ref, o_ref):
        x = seg(i)
        for t in range(NA_WIDTH // LANES):
            cols = slice(t * LANES, (t + 1) * LANES)
            xt = x[:, cols]
            x2 = xt * xt
            sa = jnp.sum(jnp.where(first, x2, 0.0), axis=-1, keepdims=True)
            sb = jnp.sum(jnp.where(first, 0.0, x2), axis=-1, keepdims=True)
            ms = jnp.where(first, sa, sb) * (1.0 / NA_DH)
            o_ref[:, cols] = ((xt * lax.rsqrt(ms + EPS)) * g_ref[:, cols]).astype(BF16)

    decay(1, lbf_ref, o["kf"], o["gfh"], o["gfl"])
    decay(2, lbb_ref, o["kb"], o["gbh"], o["gbl"])
    head_norm(5, qg_ref, o["nq"])
    head_norm(6, kg_ref, o["nk"])
    o["q"][...] = _silu(seg(0)).astype(BF16)
    o["sg"][...] = _silu(seg(4)).astype(BF16)
    o["v"][...] = seg(3).astype(BF16)
    o["nv"][...] = seg(7).astype(BF16)


def _in_proj(x, w, l, *, seq, b0):
    n = x.shape[0]
    vec = _layer_spec((1, HG_WIDTH), l)
    outs = pl.pallas_call(
        _proj_kernel,
        grid=(n // TM_PROJ,),
        in_specs=[
            pl.BlockSpec((TM_PROJ, D_MODEL), lambda i: (i, 0)),
            _mod_spec(l, b0, seq, TM_PROJ),
            _layer_spec((1, D_MODEL), l, 1),
            _layer_spec((D_MODEL, IN_WIDTH), l),
            vec, vec, vec, vec,
        ],
        out_specs=[pl.BlockSpec((TM_PROJ, HG_WIDTH), lambda i: (i, 0))] * len(PROJ_OUTS),
        out_shape=[jax.ShapeDtypeStruct((n, HG_WIDTH), BF16)] * len(PROJ_OUTS),
        compiler_params=_params("arbitrary"),
        name="in_proj",
    )(x, w["mod"], w["norm_g"], w["w_in"], w["lb_f"], w["lb_b"], w["qg"], w["kg"])
    return dict(zip(PROJ_OUTS, outs))


def _hgrn_constants():
    c = HG_C
    idx = np.arange(c)
    t = idx[None, :]
    i = idx[:, None]
    run_blocks = [t <= i]
    sels, masks = [], []
    for s in HG_LEVELS:
        half = s // 2
        p = idx % s
        start = idx - p
        m = (start + half - 1)[:, None]
        is_q = p >= half
        masks.append(is_q[:, None] & ~is_q[None, :] & (start[:, None] == start[None, :]))
        if s in HG_FINE:
            qrow = (t >= m + 1) & (t <= i)
            krow = (t >= i + 1) & (t <= m)
            run_blocks.append(np.where(is_q[:, None], qrow, krow))
            sels.append(np.broadcast_to(is_q[:, None], (c, LANES)))
    masks.append(np.eye(c, dtype=bool))
    assert len(masks) % 2 == 0
    both = lambda a: np.stack([a, a[:, ::-1, ::-1]])
    run = both(np.stack(run_blocks).astype(np.float32))
    run = np.concatenate([run, run], axis=-1).reshape(2, -1, 2 * c)
    msk = both(np.stack(masks).astype(np.float32))
    msk = msk.reshape(2, len(masks) // 2, 2, c, c).transpose(0, 1, 3, 2, 4).reshape(2, -1, c, 2 * c)
    sel = np.stack(sels).astype(np.float32)
    sel = np.stack([sel, sel[:, ::-1, :]])
    return jnp.asarray(run, dtype=BF16), jnp.asarray(msk, dtype=BF16), jnp.asarray(sel, dtype=F32)


def _hgrn_operands(d, b, a_fine, q, kk, kkb, selq, x_ref, xt_ref, e_ref):
    c = HG_C
    nlev = len(HG_LEVELS)

    def put(l, x):
        xb = x.astype(BF16)
        x_ref[l] = xb
        xt_ref[l] = xb.T

    for l, s in enumerate(HG_COARSE):
        half = s // 2
        parts = []
        for blk in range(c // half):
            rows = slice(blk * half, (blk + 1) * half)
            base = (blk // 2) * s
            if d == 0:
                m, is_q = base + half - 1, blk % 2 == 1
            else:
                m, is_q = base + half, blk % 2 == 0
            bm = b[m:m + 1]
            if is_q:
                parts.append(q[rows] * jnp.exp2(b[rows] - bm))
            else:
                parts.append(kk[rows] * jnp.exp2(bm - b[rows]))
        put(l, jnp.concatenate(parts, axis=0))
    for lf in range(len(HG_FINE)):
        e = jnp.exp2(a_fine[lf * c:(lf + 1) * c])
        put(len(HG_COARSE) + lf, jnp.where(selq[lf], q, kk) * e)
    last = c - 1 if d == 0 else 0
    b_tot = b[last:last + 1]
    x_ref[nlev] = (q * jnp.exp2(b)).astype(BF16)
    xt_ref[nlev] = kkb.T
    xt_ref[nlev + 1] = (kk * jnp.exp2(b_tot - b)).astype(BF16).T
    e_ref[...] = jnp.broadcast_to(jnp.exp2(b_tot), e_ref.shape)


def _mixer_kernel(qf_ref, kf_ref, gfh_ref, gfl_ref, vf_ref, qb_ref, kb_ref, gbh_ref, gbl_ref, vb_ref,
                  nqf_ref, nkf_ref, ngfh_ref, ngfl_ref, nqb_ref, nkb_ref, ngbh_ref, ngbl_ref,
                  aq_ref, ak_ref, av_ref, bias_ref,
                  run_ref, msk_ref, sel_ref, of_ref, ob_ref, ona_ref, st_ref,
                  a0_ref, a1_ref, x0_ref, x1_ref, xt0_ref, xt1_ref, e0_ref, e1_ref, ns_ref, *, seq):
    c = HG_C
    nchunk = HG_R // c
    assert nchunk % 2 == 0 and nchunk >= 2 * HG_AHEAD
    na = _na_stage_fns(aq_ref, ak_ref, av_ref, bias_ref, ona_ref, ns_ref, seq)
    nlev = len(HG_LEVELS)
    npair = (nlev + 1) // 2
    a_refs, x_refs, e_refs = (a0_ref, a1_ref), (x0_ref, x1_ref), (e0_ref, e1_ref)
    xt_refs = (xt0_ref, xt1_ref)

    cur = ((qf_ref, kf_ref, gfh_ref, gfl_ref), (qb_ref, kb_ref, gbh_ref, gbl_ref))
    nxt = ((nqf_ref, nkf_ref, ngfh_ref, ngfl_ref), (nqb_ref, nkb_ref, ngbh_ref, ngbl_ref))
    vo = ((vf_ref, of_ref), (vb_ref, ob_ref))
    zero = jnp.zeros((c, LANES), BF16)

    def chunk_rows(j, d):
        jj = j if d == 0 else nchunk - 1 - j
        return pl.ds(pl.multiple_of(jj * c, c), c)

    def cur_src(j):
        return lambda d: (cur[d], chunk_rows(j, d))

    def nxt_src(jn):
        return lambda d: (nxt[d], pl.ds((jn if d == 0 else HG_AHEAD - 1 - jn) * c, c))

    def run_sums(src, slot):
        for d in range(2):
            (_, _, gh_ref, gl_ref), rows = src(d)
            for hp in range(HG_HEADS // 2):
                cols2 = slice(hp * 2 * LANES, (hp + 1) * 2 * LANES)
                g2 = jnp.concatenate([gh_ref[rows, cols2], gl_ref[rows, cols2]], axis=0)
                a_refs[slot][2 * d + hp] = _dot(run_ref[d], g2)

    def operands(src, slot, d, h, selq):
        (q_ref, k_ref, _, _), rows = src(d)
        cols = slice(h * LANES, (h + 1) * LANES)
        lanes = slice((h % 2) * LANES, (h % 2 + 1) * LANES)
        a_ref = a_refs[slot]
        ci = d * HG_HEADS + h
        kkb = k_ref[rows, cols]
        _hgrn_operands(d, a_ref[2 * d + h // 2, 0:c, lanes], a_ref[2 * d + h // 2, c:, lanes],
                       q_ref[rows, cols].astype(F32), kkb.astype(F32), kkb, selq,
                       x_refs[slot].at[ci], xt_refs[slot].at[ci], e_refs[slot].at[ci])

    def level_pair(j, slot, d, h, i):
        q_ref = cur[d][0]
        rows = chunk_rows(j, d)
        cols = slice(h * LANES, (h + 1) * LANES)
        ci = d * HG_HEADS + h
        x_ref, xt_ref = x_refs[slot].at[ci], xt_refs[slot].at[ci]
        l2 = x_ref[2 * i + 1] if 2 * i + 1 < nlev else q_ref[rows, cols]
        g2 = _dot(jnp.concatenate([x_ref[2 * i], l2], axis=1),
                  jnp.concatenate([jnp.concatenate([xt_ref[2 * i], zero], axis=1),
                                   jnp.concatenate([zero, xt_ref[2 * i + 1]], axis=1)], axis=0))
        g2 = g2.astype(BF16) * msk_ref[d, i]
        return g2[:, :c] + g2[:, c:]

    def finish(j, slot, d, h, p):
        v_ref, o_ref = vo[d]
        rows = chunk_rows(j, d)
        cols = slice(h * LANES, (h + 1) * LANES)
        ci = d * HG_HEADS + h
        vb = v_ref[rows, cols]
        st = st_ref[d, h]
        o_ref[rows, cols] = _dot(jnp.concatenate([p, x_refs[slot][ci, nlev]], axis=1),
                                 jnp.concatenate([vb, st.astype(BF16)], axis=0)).astype(o_ref.dtype)
        e_col = jnp.transpose(jnp.broadcast_to(e_refs[slot][ci, 0:1, :], (c, LANES)))
        st_ref[d, h] = st * e_col + _dot(xt_refs[slot][ci, nlev + 1], vb)

    def sel_masks():
        return [[sel_ref[d, lf] != 0.0 for lf in range(len(HG_FINE))] for d in range(2)]

    chains = [(d, h) for d in range(2) for h in range(HG_HEADS)]
    assert len(chains) % npair == 0

    def step(j, slot, ahead2, ahead1):
        run_sums(ahead2, slot)
        na_scores, na_softmax, na_values = na
        na_scores(j)
        selq = sel_masks()
        ps = [None] * len(chains)
        per = len(chains) // npair
        probs = None
        for i in range(npair):
            for ci, (d, h) in enumerate(chains):
                g = level_pair(j, slot, d, h, i)
                ps[ci] = g if ps[ci] is None else ps[ci] + g
            for d, h in chains[i * per:(i + 1) * per]:
                operands(ahead1, 1 - slot, d, h, selq[d])
            if i == npair // 2 - 1:
                probs = na_softmax()
        na_values(j, probs)
        for ci, (d, h) in enumerate(chains):
            finish(j, slot, d, h, ps[ci])

    def body(i, carry):
        step(2 * i, 0, cur_src(2 * i + 2), cur_src(2 * i + 1))
        step(2 * i + 1, 1, cur_src(2 * i + 3), cur_src(2 * i + 2))
        return carry

    @pl.when(pl.program_id(1) == 0)
    def _():
        st_ref[...] = jnp.zeros_like(st_ref)

    @pl.when(jnp.logical_and(pl.program_id(0) == 0, pl.program_id(1) == 0))
    def _():
        run_sums(cur_src(0), 0)
        run_sums(cur_src(1), 1)
        selq = sel_masks()
        for d, h in chains:
            operands(cur_src(0), 0, d, h, selq[d])

    lax.fori_loop(0, nchunk // 2 - 1, body, 0)
    step(nchunk - 2, 0, nxt_src(0), cur_src(nchunk - 1))
    step(nchunk - 1, 1, nxt_src(1), nxt_src(0))


def _mixer(p, w, l, consts, *, seq):
    n = p["q"].shape[0]
    nb, nt = n // seq, seq // HG_R
    rows = seq // GRID_W
    assert nt * NA_STEP_ROWS == (NA_HEADS // 2) * rows and rows % NA_STEP_ROWS == 0
    run, msk, sel = consts
    head_pair = lambda t: (t * NA_STEP_ROWS) // rows
    col = pl.BlockSpec((seq, LANES), lambda b, t: (b, head_pair(t)))
    bias = pl.BlockSpec((None, 1, 2, 2 * WIN_ROWS - 2, GRID_W, LANES),
                        lambda b, t: (l, head_pair(t), 0, 0, 0, 0))
    ahead = HG_AHEAD * HG_C
    per = HG_R // ahead
    fwd = pl.BlockSpec((HG_R, HG_WIDTH), lambda b, t: (b * nt + t, 0))
    bwd = pl.BlockSpec((HG_R, HG_WIDTH), lambda b, t: (b * nt + nt - 1 - t, 0))
    def next_step(b, t):
        s = jnp.minimum(b * nt + t + 1, nb * nt - 1)
        return s // nt, s % nt

    def nfwd_map(b, t):
        b1, t1 = next_step(b, t)
        return ((b1 * nt + t1) * per, 0)

    def nbwd_map(b, t):
        b1, t1 = next_step(b, t)
        return ((b1 * nt + nt - 1 - t1) * per + per - 1, 0)

    nfwd = pl.BlockSpec((ahead, HG_WIDTH), nfwd_map)
    nbwd = pl.BlockSpec((ahead, HG_WIDTH), nbwd_map)
    out = jax.ShapeDtypeStruct((n, HG_WIDTH), BF16)
    return pl.pallas_call(
        functools.partial(_mixer_kernel, seq=seq),
        grid=(nb, nt),
        in_specs=[fwd] * 5 + [bwd] * 5 + [nfwd] * 4 + [nbwd] * 4 + [col, col, col, bias]
        + [_const_spec(run.shape), _const_spec(msk.shape), _const_spec(sel.shape)],
        out_specs=[fwd, bwd, col],
        out_shape=[out, out, jax.ShapeDtypeStruct((n, NA_WIDTH), BF16)],
        scratch_shapes=[pltpu.VMEM((2, HG_HEADS, HG_DK, LANES), F32)]
        + [pltpu.VMEM((HG_GROUPS, run.shape[1], 2 * LANES), F32)] * 2
        + [pltpu.VMEM((HG_CHAINS, len(HG_LEVELS) + 1, HG_C, LANES), BF16)] * 2
        + [pltpu.VMEM((HG_CHAINS, len(HG_LEVELS) + 2, HG_DK, HG_C), BF16)] * 2
        + [pltpu.VMEM((HG_CHAINS, SUBLANES, LANES), F32)] * 2
        + [pltpu.VMEM((NA_ROWS_PER_ITER, 2 * GRID_W, WIN_ROWS * GRID_W), F32)],
        compiler_params=_params("arbitrary", "arbitrary"),
        name="mixer",
    )(p["q"], p["kf"], p["gfh"], p["gfl"], p["v"], p["q"], p["kb"], p["gbh"], p["gbl"], p["v"],
      p["q"], p["kf"], p["gfh"], p["gfl"], p["q"], p["kb"], p["gbh"], p["gbl"],
      p["nq"], p["nk"], p["nv"], w["bias"], run, msk, sel)


def _na_stage_fns(q_ref, k_ref, v_ref, bias_ref, o_ref, s_ref, seq):
    rows = seq // GRID_W
    win = WIN_ROWS * GRID_W
    assert rows & (rows - 1) == 0
    row_base = jnp.bitwise_and(pl.program_id(1) * NA_STEP_ROWS, rows - 1)
    lane = lax.broadcasted_iota(jnp.int32, (GRID_W, LANES), 1)
    first = lane < NA_DH

    def geometry(j):
        rs = [row_base + j * NA_ROWS_PER_ITER + u for u in range(NA_ROWS_PER_ITER)]
        r0s = [jnp.clip(r - WIN_ROWS // 2, 0, rows - WIN_ROWS) for r in rs]
        qrows = [pl.ds(pl.multiple_of(r * GRID_W, GRID_W), GRID_W) for r in rs]
        wrows = [pl.ds(pl.multiple_of(r0 * GRID_W, GRID_W), win) for r0 in r0s]
        return rs, r0s, qrows, wrows

    def scores(j):
        rs, r0s, qrows, wrows = geometry(j)
        for u, (r, r0, qr_, wr_) in enumerate(zip(rs, r0s, qrows, wrows)):
            qr = q_ref[qr_, :]
            zero = jnp.zeros_like(qr)
            q2 = jnp.concatenate([jnp.where(first, qr, zero), jnp.where(first, zero, qr)], axis=0)
            rho0 = WIN_ROWS - 1 - (r - r0)
            bias = jnp.concatenate(
                [jnp.concatenate([bias_ref[0, a, rho0 + 2 * m] for m in range(WIN_ROWS // 2)], axis=1)
                 for a in range(2)], axis=0)
            s_ref[u] = _dot_nt(q2, k_ref[wr_, :]) + bias

    def softmax():
        ps, ls = [], []
        for u in range(NA_ROWS_PER_ITER):
            s = s_ref[u]
            p = jnp.exp2(s - jnp.max(s, axis=-1, keepdims=True))
            ls.append(jnp.sum(p, axis=-1, keepdims=True))
            ps.append(p.astype(BF16))
        return ps, ls

    def values(j, probs):
        _, _, qrows, wrows = geometry(j)
        for p, l, qr_, wr_ in zip(*probs, qrows, wrows):
            o2 = _dot(p, v_ref[wr_, :]) * (1.0 / l)
            o_ref[qr_, :] = jnp.where(first, o2[:GRID_W], o2[GRID_W:]).astype(BF16)

    return scores, softmax, values


def _na_bias_tables(na_rpb):
    qcol = np.arange(GRID_W)
    c0 = np.clip(qcol - WIN_COLS // 2, 0, GRID_W - WIN_COLS)
    col_mask = (qcol[None, :] >= c0[:, None]) & (qcol[None, :] < c0[:, None] + WIN_COLS)
    rel_c = np.clip(qcol[None, :] - qcol[:, None], -(WIN_COLS - 1), WIN_COLS - 1) + WIN_COLS - 1
    onehot = (rel_c[:, :, None] == np.arange(2 * WIN_COLS - 1)).astype(np.float32)
    t = jnp.einsum("qkc,lhrc->lhrqk", onehot, na_rpb.astype(F32), precision=lax.Precision.HIGHEST)
    t = jnp.where(col_mask, t * LOG2E, NEG_BIG)
    pairs = jnp.concatenate([t[:, :, :-1], t[:, :, 1:]], axis=-1)
    return pairs.reshape(DEPTH, NA_HEADS // 2, 2, 2 * WIN_ROWS - 2, GRID_W, LANES)


def _lower_bounds(p):
    sm = jax.nn.softmax(p.astype(F32), axis=0)
    return jnp.cumsum(sm, axis=0) - sm[0]


def _trunk(x, b0, w, consts):
    nb, seq, _ = x.shape
    x = x.reshape(nb * seq, D_MODEL)
    for l in range(DEPTH):
        x = _ffn(x, w, l, 0, seq=seq, b0=b0)
        p = _in_proj(x, w, l, seq=seq, b0=b0)
        of, ob, ona = _mixer(p, w, l, consts, seq=seq)
        x = _ffn(x, w, l, 1, seq=seq, b0=b0, mix=(of, ob, p["sg"], ona))
    return x.reshape(nb, seq, D_MODEL)


def kernel(x_prompt, x_sample, c_prompt, c_sample, w_mod, b_mod, norm_g, ffn_w_gate, ffn_w_up, ffn_w_down,
           w_in, w_out, hg_lb_fwd, hg_lb_bwd, hg_norm_g, na_q_norm_g, na_k_norm_g, na_rpb):
    nbp = x_prompt.shape[0]
    c_all = jnp.concatenate([c_prompt, c_sample], axis=0)
    per_head = lambda g, scale: jnp.tile(g.astype(F32) * scale, (1, NA_HEADS)).reshape(DEPTH, 1, NA_WIDTH)
    w = {
        "mod": _modulation(c_all, w_mod, b_mod).reshape(DEPTH, c_all.shape[0], 9, D_MODEL),
        "norm_g": norm_g.reshape(DEPTH, 3, 1, D_MODEL),
        "wg": ffn_w_gate.astype(BF16), "wu": ffn_w_up.astype(BF16), "wd": ffn_w_down.astype(BF16),
        "w_in": w_in.astype(BF16), "w_out": w_out.astype(BF16),
        "lb_f": _lower_bounds(hg_lb_fwd).reshape(DEPTH, 1, HG_WIDTH),
        "lb_b": _lower_bounds(hg_lb_bwd).reshape(DEPTH, 1, HG_WIDTH),
        "hg_norm_g": hg_norm_g.reshape(DEPTH, 1, LANES),
        "qg": per_head(na_q_norm_g, NA_DH ** -0.5 * LOG2E), "kg": per_head(na_k_norm_g, 1.0),
        "bias": _na_bias_tables(na_rpb),
    }
    consts = _hgrn_constants()
    y_prompt = _trunk(x_prompt, 0, w, consts)
    y_sample = _trunk(x_sample, nbp, w, consts)
    return (y_prompt, y_sample)
```
